```python
import jax, jax.numpy as jnp
from jax import lax
import numpy as np


D_MODEL = 1024
BATCH = 8
SEQ = 4096
DEPTH = 4
DEC_BATCH = 32
DEC_SEQ = 32
PAST_LEN = 2048

CHUNK = 64
D_MIX = D_MODEL
CONV_W = D_MIX // 2
CONV_GROUPS = 8
CONV_GROUP_DIM = CONV_W // CONV_GROUPS
CONV_K = 3
M_HEADS = 4
M_DV = (D_MIX - CONV_W) // M_HEADS
M_DK = M_DV // 2
D_FF = ((8 * D_MODEL // 3 + 127) // 128) * 128
N_IN = 3 * CONV_W + 2 * M_HEADS * M_DK + 2 * M_HEADS * M_DV + 2 * M_HEADS
EPS = 1e-6

kernel_name = 'hybrid_conv_mlstm_macaron_stream_step'


def _rmsnorm(x, g):
    xf = x.astype(jnp.float32)
    y = xf * lax.rsqrt(jnp.mean(xf * xf, axis=-1, keepdims=True) + EPS)
    return (y * g.astype(jnp.float32)).astype(x.dtype)


def _swiglu(x, w_gate, w_up, w_down):
    return (jax.nn.silu(x @ w_gate) * (x @ w_up)) @ w_down


def _mlstm_chunk(carry, inp):
    C, n, m = carry
    q, k, v, li, lf = inp
    L = q.shape[2]
    b = jnp.cumsum(lf, axis=-1)
    causal = jnp.tril(jnp.ones((L, L), dtype=bool))
    d_log = b[..., :, None] - b[..., None, :] + li[..., None, :]
    d_log = jnp.where(causal, d_log, -jnp.inf)
    inter = b + m[..., None]
    m_t = jnp.maximum(inter, jnp.max(d_log, axis=-1))
    w_intra = jnp.exp(d_log - m_t[..., None])
    w_inter = jnp.exp(inter - m_t)
    s = jnp.einsum('bhtd,bhsd->bhts', q, k) * w_intra
    num = jnp.einsum('bhts,bhsv->bhtv', s, v) + w_inter[..., None] * jnp.einsum('bhtd,bhdv->bhtv', q, C)
    den = jnp.sum(s, axis=-1) + w_inter * jnp.einsum('bhtd,bhd->bht', q, n)
    h = num / jnp.maximum(jnp.abs(den), jnp.exp(-m_t))[..., None]
    b_last = b[..., -1]
    g = b_last[..., None] - b + li
    m_new = jnp.maximum(b_last + m, jnp.max(g, axis=-1))
    wk = jnp.exp(g - m_new[..., None])
    decay = jnp.exp(b_last + m - m_new)
    C_new = decay[..., None, None] * C + jnp.einsum('bhs,bhsd,bhsv->bhdv', wk, k, v)
    n_new = decay[..., None] * n + jnp.einsum('bhs,bhsd->bhd', wk, k)
    return (C_new, n_new, m_new), h


def _mlstm(q, k, v, li, lf, C0, n0, m0):
    bsz, T = q.shape[0], q.shape[1]
    L = min(CHUNK, T)
    nc = T // L
    f32 = jnp.float32

    def to_chunks(a):
        a = a.astype(f32).reshape((bsz, nc, L) + a.shape[2:])
        return jnp.moveaxis(jnp.swapaxes(a, 2, 3), 1, 0)

    xs = (to_chunks(q) * (M_DK ** -0.5), to_chunks(k), to_chunks(v), to_chunks(li), to_chunks(lf))
    carry0 = (C0.astype(f32), n0.astype(f32), m0.astype(f32))
    (C, n, m), h = lax.scan(_mlstm_chunk, carry0, xs)
    h = jnp.swapaxes(jnp.moveaxis(h, 0, 1), 2, 3).reshape(bsz, T, M_HEADS, M_DV)
    return h, C, n, m


def _mixer(h, conv_prev, C0, n0, m0, w_in, b_igate, b_fgate, conv_w, conv_b, g_conv, g_mlstm, w_out):
    bsz, T, _ = h.shape
    sizes = [CONV_W, CONV_W, CONV_W, M_HEADS * M_DK, M_HEADS * M_DK, M_HEADS * M_DV, M_HEADS * M_DV, M_HEADS, M_HEADS]
    offs = np.cumsum(sizes)[:-1].tolist()
    p = h @ w_in
    bg, cg, xin, q, k, v, o, ip, fp = jnp.split(p, offs, axis=-1)
    u = cg * xin
    u_pad = jnp.concatenate([conv_prev.astype(u.dtype), u], axis=1)
    y = conv_b + conv_w[0] * u_pad[:, :-2] + conv_w[1] * u_pad[:, 1:-1] + conv_w[2] * u_pad[:, 2:]
    conv_new = u_pad[:, -(CONV_K - 1):]
    a_out = _rmsnorm((bg * y).reshape(bsz, T, CONV_GROUPS, CONV_GROUP_DIM), g_conv).reshape(bsz, T, CONV_W)
    li = ip + b_igate
    lf = jax.nn.log_sigmoid((fp + b_fgate).astype(jnp.float32))
    hm, C, n, m = _mlstm(q.reshape(bsz, T, M_HEADS, M_DK), k.reshape(bsz, T, M_HEADS, M_DK),
                         v.reshape(bsz, T, M_HEADS, M_DV), li, lf, C0, n0, m0)
    hm = _rmsnorm(hm.astype(h.dtype), g_mlstm) * jax.nn.sigmoid(o).reshape(bsz, T, M_HEADS, M_DV)
    mix = jnp.concatenate([a_out, hm.reshape(bsz, T, M_HEADS * M_DV)], axis=-1)
    return mix @ w_out, conv_new, C, n, m


def _trunk(x, conv0, C0, n0, m0, ffn1_norm, ffn1_w_gate, ffn1_w_up, ffn1_w_down, mix_norm, w_in,
           b_igate, b_fgate, conv_w, conv_b, g_conv, g_mlstm, w_out, ffn2_norm, ffn2_w_gate,
           ffn2_w_up, ffn2_w_down, final_norm):
    convs, Cs, ns, ms = [], [], [], []
    for l in range(DEPTH):
        x = x + 0.5 * _swiglu(_rmsnorm(x, ffn1_norm[l]), ffn1_w_gate[l], ffn1_w_up[l], ffn1_w_down[l])
        y, cv, C, n, m = _mixer(_rmsnorm(x, mix_norm[l]), conv0[l], C0[l], n0[l], m0[l], w_in[l],
                                b_igate[l], b_fgate[l], conv_w[l], conv_b[l], g_conv[l], g_mlstm[l], w_out[l])
        x = x + y
        x = x + 0.5 * _swiglu(_rmsnorm(x, ffn2_norm[l]), ffn2_w_gate[l], ffn2_w_up[l], ffn2_w_down[l])
        convs.append(cv)
        Cs.append(C)
        ns.append(n)
        ms.append(m)
    return _rmsnorm(x, final_norm), jnp.stack(convs), jnp.stack(Cs), jnp.stack(ns), jnp.stack(ms)


def setup_inputs(seed: int = 0) -> dict:
    key = jax.random.key(seed)
    ks = jax.random.split(key, 32)
    nrm = lambda k, shape, s: jax.random.normal(k, shape, jnp.float32) * s
    gain = lambda k, shape: 1.0 + 0.05 * jax.random.normal(k, shape, jnp.float32)
    b_f = jnp.linspace(3.0, 6.0, M_HEADS, dtype=jnp.float32)[None, :] + nrm(ks[13], (DEPTH, M_HEADS), 0.1)
    return {
        'x_prompt': nrm(ks[0], (BATCH, SEQ, D_MODEL), 1.0),
        'x_sample': nrm(ks[1], (DEC_BATCH, DEC_SEQ, D_MODEL), 1.0),
        'cache_conv': nrm(ks[2], (DEPTH, DEC_BATCH, CONV_K - 1, CONV_W), 1.0),
        'state_C': nrm(ks[3], (DEPTH, DEC_BATCH, M_HEADS, M_DK, M_DV), 0.1),
        'state_n': jnp.abs(nrm(ks[4], (DEPTH, DEC_BATCH, M_HEADS, M_DK), 0.5)),
        'state_m': nrm(ks[5], (DEPTH, DEC_BATCH, M_HEADS), 0.5),
        'ffn1_norm': gain(ks[6], (DEPTH, D_MODEL)),
        'ffn1_w_gate': nrm(ks[7], (DEPTH, D_MODEL, D_FF), D_MODEL ** -0.5),
        'ffn1_w_up': nrm(ks[8], (DEPTH, D_MODEL, D_FF), D_MODEL ** -0.5),
        'ffn1_w_down': nrm(ks[9], (DEPTH, D_FF, D_MODEL), D_FF ** -0.5),
        'mix_norm': gain(ks[10], (DEPTH, D_MODEL)),
        'w_in': nrm(ks[11], (DEPTH, D_MODEL, N_IN), D_MODEL ** -0.5),
        'b_igate': nrm(ks[12], (DEPTH, M_HEADS), 0.1),
        'b_fgate': b_f,
        'conv_w': nrm(ks[14], (DEPTH, CONV_K, CONV_W), 0.5),
        'conv_b': nrm(ks[15], (DEPTH, CONV_W), 0.01),
        'g_conv': gain(ks[16], (DEPTH, CONV_GROUPS, CONV_GROUP_DIM)),
        'g_mlstm': gain(ks[17], (DEPTH, M_HEADS, M_DV)),
        'w_out': nrm(ks[18], (DEPTH, D_MIX, D_MODEL), D_MIX ** -0.5),
        'ffn2_norm': gain(ks[19], (DEPTH, D_MODEL)),
        'ffn2_w_gate': nrm(ks[20], (DEPTH, D_MODEL, D_FF), D_MODEL ** -0.5),
        'ffn2_w_up': nrm(ks[21], (DEPTH, D_MODEL, D_FF), D_MODEL ** -0.5),
        'ffn2_w_down': nrm(ks[22], (DEPTH, D_FF, D_MODEL), D_FF ** -0.5),
        'final_norm': gain(ks[23], (D_MODEL,)),
    }


def reference(x_prompt, x_sample, cache_conv, state_C, state_n, state_m, ffn1_norm, ffn1_w_gate,
              ffn1_w_up, ffn1_w_down, mix_norm, w_in, b_igate, b_fgate, conv_w, conv_b, g_conv,
              g_mlstm, w_out, ffn2_norm, ffn2_w_gate, ffn2_w_up, ffn2_w_down, final_norm):
    bsz = x_prompt.shape[0]
    conv0 = jnp.zeros((DEPTH, bsz, CONV_K - 1, CONV_W), x_prompt.dtype)
    C0 = jnp.zeros((DEPTH, bsz, M_HEADS, M_DK, M_DV), jnp.float32)
    n0 = jnp.zeros((DEPTH, bsz, M_HEADS, M_DK), jnp.float32)
    m0 = jnp.zeros((DEPTH, bsz, M_HEADS), jnp.float32)
    y_prompt, p_conv, p_C, p_n, p_m = _trunk(
        x_prompt, conv0, C0, n0, m0, ffn1_norm, ffn1_w_gate, ffn1_w_up, ffn1_w_down, mix_norm, w_in,
        b_igate, b_fgate, conv_w, conv_b, g_conv, g_mlstm, w_out, ffn2_norm, ffn2_w_gate, ffn2_w_up,
        ffn2_w_down, final_norm)
    y_sample, s_conv, s_C, s_n, s_m = _trunk(
        x_sample, cache_conv, state_C, state_n, state_m, ffn1_norm, ffn1_w_gate, ffn1_w_up, ffn1_w_down,
        mix_norm, w_in, b_igate, b_fgate, conv_w, conv_b, g_conv, g_mlstm, w_out, ffn2_norm, ffn2_w_gate,
        ffn2_w_up, ffn2_w_down, final_norm)
    return (y_prompt, y_sample, p_conv, p_C, p_n, p_m, s_conv, s_C, s_n, s_m)
```

```python
import functools

import jax
import jax.numpy as jnp
from jax import lax
from jax.experimental import pallas as pl
from jax.experimental.pallas import tpu as pltpu

EPS = 1e-6
LANES = 128
VMEM_LIMIT_BYTES = 56 * 1024 * 1024

_BF16 = jnp.bfloat16
_F32 = jnp.float32


def _dot(a, b):
    return jnp.dot(a, b, preferred_element_type=_F32)


def _dot_nt(a, b):
    return lax.dot_general(a, b, (((1,), (1,)), ((), ())), preferred_element_type=_F32)


def _ffn_kernel(x_ref, g_ref, wg_ref, wu_ref, wd_ref, fg_ref, o_ref, *, final_norm):
    x = x_ref[...]
    ms = jnp.mean(x * x, axis=-1, keepdims=True)
    hn = (x * lax.rsqrt(ms + EPS) * g_ref[...]).astype(_BF16)
    gate = _dot(hn, wg_ref[...])
    up = _dot(hn, wu_ref[...])
    act = (gate * jax.nn.sigmoid(gate) * up).astype(_BF16)
    y = x + 0.5 * _dot(act, wd_ref[...])
    if final_norm:
        ms2 = jnp.mean(y * y, axis=-1, keepdims=True)
        y = y * lax.rsqrt(ms2 + EPS) * fg_ref[...]
    o_ref[...] = y


def _ffn(x, g, wg, wu, wd, fg, *, tm, final_norm):
    n, d = x.shape
    dff = wg.shape[1]
    const = lambda i: (0, 0)
    return pl.pallas_call(
        functools.partial(_ffn_kernel, final_norm=final_norm),
        out_shape=jax.ShapeDtypeStruct((n, d), _F32),
        grid=(n // tm,),
        in_specs=[
            pl.BlockSpec((tm, d), lambda i: (i, 0)),
            pl.BlockSpec((1, d), const),
            pl.BlockSpec((d, dff), const, pipeline_mode=pl.Buffered(1)),
            pl.BlockSpec((d, dff), const, pipeline_mode=pl.Buffered(1)),
            pl.BlockSpec((dff, d), const, pipeline_mode=pl.Buffered(1)),
            pl.BlockSpec((1, d), const),
        ],
        out_specs=pl.BlockSpec((tm, d), lambda i: (i, 0)),
        compiler_params=pltpu.CompilerParams(
            dimension_semantics=("arbitrary",), vmem_limit_bytes=VMEM_LIMIT_BYTES),
        name="ffn_final" if final_norm else "ffn",
    )(x, g, wg, wu, wd, fg)


def _log_sigmoid(x):
    return jnp.minimum(x, 0.0) - jnp.log1p(jnp.exp(-jnp.abs(x)))


def _pad_rows(a, rows):
    if a.shape[0] == rows:
        return a
    return jnp.concatenate([a, jnp.zeros((rows - a.shape[0], a.shape[1]), a.dtype)], axis=0)


def _mixer_kernel(x_ref, conv0_ref, cn0_ref, m0_ref, nrm_ref, win_ref, gbias_ref, convw_ref,
                  convb_ref, gconv_ref, gml_ref, wout_ref,
                  xo_ref, convo_ref, cno_ref, mo_ref,
                  ubuf, cn_scr, m_scr, mixbuf,
                  *, tm, chunk, n_slots, heads, dk, dv, conv_w, group_dim):
    t = pl.program_id(1)
    n_t = pl.num_programs(1)
    seg = tm // n_slots
    n_sub = seg // chunk
    lc = max(chunk, LANES)
    pairs = heads // 2
    qk_w = heads * dk

    @pl.when(t == 0)
    def _init():
        for s in range(n_slots):
            ubuf[s, 6:8, :] = conv0_ref[s]
        cn_scr[...] = cn0_ref[...]
        m_scr[...] = m0_ref[...]

    x = x_ref[...]
    ms = jnp.mean(x * x, axis=-1, keepdims=True)
    hn = (x * lax.rsqrt(ms + EPS) * nrm_ref[...]).astype(_BF16)

    c0 = 3 * conv_w
    p_conv = _dot(hn, win_ref[:, 0:c0])
    q_all = _dot(hn, win_ref[:, c0:c0 + qk_w]) * (dk ** -0.5)
    k_all = _dot(hn, win_ref[:, c0 + qk_w:c0 + 2 * qk_w])
    c1 = c0 + 2 * qk_w
    v_all = _dot(hn, win_ref[:, c1:c1 + heads * dv])
    o_all = _dot(hn, win_ref[:, c1 + heads * dv:c1 + 2 * heads * dv])
    c2 = c1 + 2 * heads * dv
    gates = _dot(hn, win_ref[:, c2:c2 + LANES]) + gbias_ref[...]

    lane = lax.broadcasted_iota(jnp.int32, (seg, LANES), 1)
    low_half = lane < group_dim
    u_all = p_conv[:, conv_w:2 * conv_w] * p_conv[:, 2 * conv_w:3 * conv_w]
    for s in range(n_slots):
        r0 = s * seg
        u = u_all[r0:r0 + seg]
        ubuf[s, 8:8 + seg, :] = u
        u1 = ubuf[s, 7:7 + seg, :]
        u2 = ubuf[s, 6:6 + seg, :]
        y = convb_ref[...] + convw_ref[0:1, :] * u2 + convw_ref[1:2, :] * u1 + convw_ref[2:3, :] * u
        ubuf[s, 6:8, :] = ubuf[s, seg + 6:seg + 8, :]
        a = p_conv[r0:r0 + seg, 0:conv_w] * y
        for jb in range(conv_w // LANES):
            blk = a[:, jb * LANES:(jb + 1) * LANES]
            sq = blk * blk
            lo = jnp.sum(jnp.where(low_half, sq, 0.0), axis=-1, keepdims=True)
            hi = jnp.sum(jnp.where(low_half, 0.0, sq), axis=-1, keepdims=True)
            msq = jnp.where(low_half, lo, hi) * (1.0 / group_dim)
            nb = blk * lax.rsqrt(msq + EPS) * gconv_ref[:, jb * LANES:(jb + 1) * LANES]
            mixbuf[r0:r0 + seg, jb * LANES:(jb + 1) * LANES] = nb.astype(_BF16)

    lf = _log_sigmoid(gates)
    rowc = lax.broadcasted_iota(jnp.int32, (tm, LANES), 0) % chunk
    bc = lf
    sh = 1
    while sh < chunk:
        bc = bc + jnp.where(rowc >= sh, pltpu.roll(bc, sh, axis=0), 0.0)
        sh *= 2

    row = lax.broadcasted_iota(jnp.int32, (chunk, lc), 0)
    col = lax.broadcasted_iota(jnp.int32, (chunk, lc), 1)
    causal = col <= row
    lane_c = lax.broadcasted_iota(jnp.int32, (chunk, LANES), 1)
    head_lo = lane_c < dk
    srow = lax.broadcasted_iota(jnp.int32, (2 * dk, 1), 0)
    ones_col = jnp.where(lax.broadcasted_iota(jnp.int32, (lc, dv), 1) == 0, 1.0, 0.0).astype(_BF16)

    for s in range(n_slots):
        for j in range(n_sub):
            r0 = s * seg + j * chunk
            g_j = gates[r0:r0 + chunk]
            b_j = bc[r0:r0 + chunk]
            g_t = _pad_rows(g_j, lc).T
            b_t = _pad_rows(b_j, lc).T
            m_all = m_scr[s]
            m_new_row = m_all
            for pr in range(pairs):
                q_pair = q_all[r0:r0 + chunk, pr * LANES:(pr + 1) * LANES]
                k_pair = k_all[r0:r0 + chunk, pr * LANES:(pr + 1) * LANES]
                k_pad = _pad_rows(k_pair, lc).astype(_BF16)
                cn_pair = cn_scr[s, pr]
                cn_bf = cn_pair.astype(_BF16)
                upd = None
                decays = []
                for hh in range(2):
                    h = 2 * pr + hh
                    hmask = head_lo if hh == 0 else jnp.logical_not(head_lo)
                    b_col = b_j[:, heads + h:heads + h + 1]
                    b_row = b_t[heads + h:heads + h + 1, :]
                    li_row = g_t[h:h + 1, :]
                    li_col = g_j[:, h:h + 1]
                    m_prev = m_all[:, h:h + 1]
                    d = jnp.where(causal, b_col - b_row + li_row, -jnp.inf)
                    inter = b_col + m_prev
                    m_t = jnp.maximum(inter, jnp.max(d, axis=-1, keepdims=True))
                    w_intra = jnp.exp(d - m_t)
                    w_inter = jnp.exp(inter - m_t)
                    qm = jnp.where(hmask, q_pair, 0.0).astype(_BF16)
                    sc = _dot_nt(qm, k_pad) * w_intra
                    v_h = v_all[r0:r0 + chunk, h * dv:(h + 1) * dv]
                    v_pad = _pad_rows(v_h, lc).astype(_BF16)
                    inter_cn = _dot(qm, cn_bf)
                    num = _dot(sc.astype(_BF16), v_pad) + w_inter * inter_cn[:, 0:dv]
                    den = jnp.sum(sc, axis=-1, keepdims=True) + w_inter * inter_cn[:, dv:dv + 1]
                    h_out = num / jnp.maximum(jnp.abs(den), jnp.exp(-m_t))
                    msq = jnp.mean(h_out * h_out, axis=-1, keepdims=True)
                    o_h = o_all[r0:r0 + chunk, h * dv:(h + 1) * dv]
                    hm = h_out * lax.rsqrt(msq + EPS) * gml_ref[:, h * dv:(h + 1) * dv] * jax.nn.sigmoid(o_h)
                    mixbuf[r0:r0 + chunk, conv_w + h * dv:conv_w + (h + 1) * dv] = hm.astype(_BF16)
                    b_last = b_col[chunk - 1:chunk, :]
                    m_new = m_t[chunk - 1:chunk, :]
                    wk = jnp.exp(b_last - b_col + li_col - m_new)
                    decays.append(jnp.exp(b_last + m_prev - m_new))
                    kw = jnp.where(hmask, k_pair * wk, 0.0)
                    kw_t = _pad_rows(kw, lc).T.astype(_BF16)
                    v_aug = jnp.concatenate([v_pad, ones_col], axis=1)
                    contrib = _dot(kw_t, v_aug)
                    upd = contrib if upd is None else upd + contrib
                    lane_m = lax.broadcasted_iota(jnp.int32, (1, LANES), 1)
                    m_new_row = jnp.where(lane_m == h, m_new, m_new_row)
                decay_rows = jnp.where(srow < dk, decays[0], decays[1])
                cn_scr[s, pr] = decay_rows * cn_pair + upd
            m_scr[s] = m_new_row

    y_mix = _dot(mixbuf[...], wout_ref[...])
    xo_ref[...] = x + y_mix

    @pl.when(t == n_t - 1)
    def _fin():
        for s in range(n_slots):
            convo_ref[s] = ubuf[s, 6:8, :]
        cno_ref[...] = cn_scr[...]
        mo_ref[...] = m_scr[...]


def _mixer(x, conv0, cn0, m0, nrm, win, gbias, convw, convb, gconv, gml, wout,
           *, n_seq, seq_len, tm, chunk, n_slots, heads, dk, dv, conv_w, group_dim):
    n, d = x.shape
    seg = tm // n_slots
    n_t = seq_len // seg
    n_b = n_seq // n_slots
    pairs = heads // 2
    nin = win.shape[1]
    const = lambda b, t: (0, 0)
    kern = functools.partial(_mixer_kernel, tm=tm, chunk=chunk, n_slots=n_slots, heads=heads,
                             dk=dk, dv=dv, conv_w=conv_w, group_dim=group_dim)
    state3 = lambda b, t: (b, 0, 0)
    state4 = lambda b, t: (b, 0, 0, 0)
    return pl.pallas_call(
        kern,
        out_shape=(
            jax.ShapeDtypeStruct((n, d), _F32),
            jax.ShapeDtypeStruct((n_seq, 2, conv_w), _F32),
            jax.ShapeDtypeStruct((n_seq, pairs, 2 * dk, 2 * dv), _F32),
            jax.ShapeDtypeStruct((n_seq, 1, LANES), _F32),
        ),
        grid=(n_b, n_t),
        in_specs=[
            pl.BlockSpec((tm, d), lambda b, t: (b * n_t + t, 0)),
            pl.BlockSpec((n_slots, 2, conv_w), state3),
            pl.BlockSpec((n_slots, pairs, 2 * dk, 2 * dv), state4),
            pl.BlockSpec((n_slots, 1, LANES), state3),
            pl.BlockSpec((1, d), const),
            pl.BlockSpec((d, nin), const, pipeline_mode=pl.Buffered(1)),
            pl.BlockSpec((1, LANES), const),
            pl.BlockSpec((3, conv_w), const),
            pl.BlockSpec((1, conv_w), const),
            pl.BlockSpec((1, conv_w), const),
            pl.BlockSpec((1, heads * dv), const),
            pl.BlockSpec((conv_w + heads * dv, d), const, pipeline_mode=pl.Buffered(1)),
        ],
        out_specs=(
            pl.BlockSpec((tm, d), lambda b, t: (b * n_t + t, 0)),
            pl.BlockSpec((n_slots, 2, conv_w), state3),
            pl.BlockSpec((n_slots, pairs, 2 * dk, 2 * dv), state4),
            pl.BlockSpec((n_slots, 1, LANES), state3),
        ),
        scratch_shapes=[
            pltpu.VMEM((n_slots, seg + 8, conv_w), _F32),
            pltpu.VMEM((n_slots, pairs, 2 * dk, 2 * dv), _F32),
            pltpu.VMEM((n_slots, 1, LANES), _F32),
            pltpu.VMEM((tm, conv_w + heads * dv), _BF16),
        ],
        compiler_params=pltpu.CompilerParams(
            dimension_semantics=("arbitrary", "arbitrary"), vmem_limit_bytes=VMEM_LIMIT_BYTES),
        name="mixer",
    )(x, conv0, cn0, m0, nrm, win, gbias, convw, convb, gconv, gml, wout)


def _prep_weights(ffn1_norm, ffn1_w_gate, ffn1_w_up, ffn1_w_down, mix_norm, w_in, b_igate, b_fgate,
                  conv_w, conv_b, g_conv, g_mlstm, w_out, ffn2_norm, ffn2_w_gate, ffn2_w_up,
                  ffn2_w_down, final_norm):
    depth, d, _ = w_in.shape
    heads = b_igate.shape[1]
    n_main = w_in.shape[2] - 2 * heads
    w_in_p = jnp.concatenate(
        [w_in[:, :, :n_main], w_in[:, :, n_main:],
         jnp.zeros((depth, d, LANES - 2 * heads), w_in.dtype)], axis=-1).astype(_BF16)
    gbias = jnp.concatenate(
        [b_igate, b_fgate, jnp.zeros((depth, LANES - 2 * heads), _F32)], axis=-1)[:, None, :]
    return dict(
        ffn1_norm=ffn1_norm[:, None, :], ffn1_w_gate=ffn1_w_gate.astype(_BF16),
        ffn1_w_up=ffn1_w_up.astype(_BF16), ffn1_w_down=ffn1_w_down.astype(_BF16),
        mix_norm=mix_norm[:, None, :], w_in=w_in_p, gbias=gbias, conv_w=conv_w,
        conv_b=conv_b[:, None, :], g_conv=g_conv.reshape(depth, 1, -1),
        g_mlstm=g_mlstm.reshape(depth, 1, -1), w_out=w_out.astype(_BF16),
        ffn2_norm=ffn2_norm[:, None, :], ffn2_w_gate=ffn2_w_gate.astype(_BF16),
        ffn2_w_up=ffn2_w_up.astype(_BF16), ffn2_w_down=ffn2_w_down.astype(_BF16),
        final_norm=final_norm[None, :],
    )


def _tile_plan(n_seq, seq_len):
    if seq_len >= 256:
        chunk = 256
        tm = 512 if seq_len % 512 == 0 else 256
        return tm, chunk, 1
    chunk = seq_len
    n_slots = max(1, min(n_seq, 256 // seq_len))
    while n_seq % n_slots:
        n_slots -= 1
    return n_slots * seq_len, chunk, n_slots


def _trunk(x, conv0, c0, n0, m0, w, dims):
    n_seq, seq_len, d = x.shape
    depth = w["w_in"].shape[0]
    heads, dk, dv, conv_w, group_dim = dims
    pairs = heads // 2
    tm, chunk, n_slots = _tile_plan(n_seq, seq_len)
    n_rows = n_seq * seq_len
    tm_ffn = 512 if n_rows % 512 == 0 else tm
    xf = x.reshape(n_rows, d)
    cn0 = jnp.concatenate(
        [c0, n0[..., None], jnp.zeros(c0.shape[:-1] + (dv - 1,), _F32)], axis=-1)
    cn0 = cn0.reshape(depth, n_seq, pairs, 2 * dk, 2 * dv)
    m0p = jnp.concatenate([m0, jnp.zeros((depth, n_seq, LANES - heads), _F32)], axis=-1)[:, :, None, :]
    convs, cns, ms = [], [], []
    for l in range(depth):
        xf = _ffn(xf, w["ffn1_norm"][l], w["ffn1_w_gate"][l], w["ffn1_w_up"][l], w["ffn1_w_down"][l],
                  w["final_norm"], tm=tm_ffn, final_norm=False)
        xf, cv, cn, m = _mixer(
            xf, conv0[l], cn0[l], m0p[l], w["mix_norm"][l], w["w_in"][l], w["gbias"][l], w["conv_w"][l],
            w["conv_b"][l], w["g_conv"][l], w["g_mlstm"][l], w["w_out"][l],
            n_seq=n_seq, seq_len=seq_len, tm=tm, chunk=chunk, n_slots=n_slots, heads=heads, dk=dk,
            dv=dv, conv_w=conv_w, group_dim=group_dim)
        xf = _ffn(xf, w["ffn2_norm"][l], w["ffn2_w_gate"][l], w["ffn2_w_up"][l], w["ffn2_w_down"][l],
                  w["final_norm"], tm=tm_ffn, final_norm=(l == depth - 1))
        convs.append(cv)
        cns.append(cn)
        ms.append(m)
    cn = jnp.stack(cns).reshape(depth, n_seq, heads, dk, 2 * dv)
    return (xf.reshape(n_seq, seq_len, d), jnp.stack(convs), cn[..., :dv], cn[..., dv],
            jnp.stack(ms)[:, :, 0, :heads])


def kernel(x_prompt, x_sample, cache_conv, state_C, state_n, state_m, ffn1_norm, ffn1_w_gate, ffn1_w_up, ffn1_w_down, mix_norm, w_in, b_igate, b_fgate, conv_w, conv_b, g_conv, g_mlstm, w_out, ffn2_norm, ffn2_w_gate, ffn2_w_up, ffn2_w_down, final_norm):
    depth, _, heads, dk, dv = state_C.shape
    cw = conv_w.shape[2]
    group_dim = g_conv.shape[2]
    assert heads % 2 == 0 and 2 * dk == LANES and dv == LANES and 2 * group_dim == LANES
    dims = (heads, dk, dv, cw, group_dim)
    w = _prep_weights(ffn1_norm, ffn1_w_gate, ffn1_w_up, ffn1_w_down, mix_norm, w_in, b_igate, b_fgate,
                      conv_w, conv_b, g_conv, g_mlstm, w_out, ffn2_norm, ffn2_w_gate, ffn2_w_up,
                      ffn2_w_down, final_norm)
    bsz = x_prompt.shape[0]
    zc = jnp.zeros((depth, bsz, 2, cw), _F32)
    zC = jnp.zeros((depth, bsz, heads, dk, dv), _F32)
    zn = jnp.zeros((depth, bsz, heads, dk), _F32)
    zm = jnp.zeros((depth, bsz, heads), _F32)
    y_p, p_conv, p_C, p_n, p_m = _trunk(x_prompt, zc, zC, zn, zm, w, dims)
    y_s, s_conv, s_C, s_n, s_m = _trunk(x_sample, cache_conv, state_C, state_n, state_m, w, dims)
    return (y_p, y_s, p_conv, p_C, p_n, p_m, s_conv, s_C, s_n, s_m)
```

```python
import functools

import jax
import jax.numpy as jnp
from jax import lax
from jax.experimental import pallas as pl
from jax.experimental.pallas import tpu as pltpu

EPS = 1e-6
LANES = 128
MXU_WIDTH = 256
VMEM_LIMIT_BYTES = 56 * 1024 * 1024

_BF16 = jnp.bfloat16
_F32 = jnp.float32


def _dot(a, b):
    return jnp.dot(a, b, preferred_element_type=_F32)


def _dot_nt(a, b):
    return lax.dot_general(a, b, (((1,), (1,)), ((), ())), preferred_element_type=_F32)


def _layer_spec(shape, layer):
    return pl.BlockSpec((None,) + shape, lambda *_: (layer, 0, 0), pipeline_mode=pl.Buffered(1))


def _ffn_kernel(x_ref, g_ref, wg_ref, wu_ref, wd_ref, fg_ref, o_ref, *, final_norm, ff_chunks):
    x = x_ref[...]
    ms = jnp.mean(x * x, axis=-1, keepdims=True)
    hn = (x * lax.rsqrt(ms + EPS) * g_ref[...]).astype(_BF16)
    y = None
    for lo, hi in ff_chunks:
        gate = _dot(hn, wg_ref[:, lo:hi])
        up = _dot(hn, wu_ref[:, lo:hi])
        act = (gate * jax.nn.sigmoid(gate) * up).astype(_BF16)
        part = _dot(act, wd_ref[lo:hi, :])
        y = part if y is None else y + part
    y = x + 0.5 * y
    if final_norm:
        ms2 = jnp.mean(y * y, axis=-1, keepdims=True)
        y = y * lax.rsqrt(ms2 + EPS) * fg_ref[...]
    o_ref[...] = y


def _ff_chunks(dff, n_chunks):
    tiles = -(-dff // MXU_WIDTH)
    bounds = [min(dff, MXU_WIDTH * ((tiles * i) // n_chunks)) for i in range(n_chunks + 1)]
    return tuple((bounds[i], bounds[i + 1]) for i in range(n_chunks) if bounds[i + 1] > bounds[i])


def _ffn(x, g, wg, wu, wd, fg, *, layer, tm, ff_split, final_norm):
    n, d = x.shape
    dff = wg.shape[2]
    return pl.pallas_call(
        functools.partial(_ffn_kernel, final_norm=final_norm, ff_chunks=_ff_chunks(dff, ff_split)),
        out_shape=jax.ShapeDtypeStruct((n, d), _F32),
        grid=(n // tm,),
        in_specs=[
            pl.BlockSpec((tm, d), lambda i: (i, 0)),
            _layer_spec((1, d), layer),
            _layer_spec((d, dff), layer),
            _layer_spec((d, dff), layer),
            _layer_spec((dff, d), layer),
            pl.BlockSpec((1, d), lambda i: (0, 0)),
        ],
        out_specs=pl.BlockSpec((tm, d), lambda i: (i, 0)),
        compiler_params=pltpu.CompilerParams(
            dimension_semantics=("arbitrary",), vmem_limit_bytes=VMEM_LIMIT_BYTES),
        name="ffn_final" if final_norm else "ffn",
    )(x, g, wg, wu, wd, fg)


def _log_sigmoid(x):
    return jnp.minimum(x, 0.0) - jnp.log1p(jnp.exp(-jnp.abs(x)))


def _pad_rows(a, rows):
    if a.shape[0] == rows:
        return a
    return jnp.concatenate([a, jnp.zeros((rows - a.shape[0], a.shape[1]), a.dtype)], axis=0)


def _project(xa_ref, nrm_ref, win_ref, gbias_ref, p_ref, *, heads, dk, dv, conv_w):
    x = xa_ref[...]
    ms = jnp.mean(x * x, axis=-1, keepdims=True)
    hn = (x * lax.rsqrt(ms + EPS) * nrm_ref[...]).astype(_BF16)
    qk_w = heads * dk
    c0 = 3 * conv_w
    c1 = c0 + 2 * qk_w
    c2 = c1 + 2 * heads * dv
    p_ref[:, 0:c0] = _dot(hn, win_ref[:, 0:c0])
    p_ref[:, c0:c0 + qk_w] = _dot(hn, win_ref[:, c0:c0 + qk_w]) * (dk ** -0.5)
    p_ref[:, c0 + qk_w:c1] = _dot(hn, win_ref[:, c0 + qk_w:c1])
    p_ref[:, c1:c2] = _dot(hn, win_ref[:, c1:c2])
    p_ref[:, c2:c2 + LANES] = _dot(hn, win_ref[:, c2:c2 + LANES]) + gbias_ref[...]


def _recurrent(p_ref, mix_ref, convw_ref, convb_ref, gconv_ref, gml_ref, ubuf, cn_scr, m_scr,
               *, tm, chunk, n_slots, heads, dk, dv, conv_w, group_dim):
    seg = tm // n_slots
    n_sub = seg // chunk
    lc = max(chunk, LANES)
    pairs = heads // 2
    qk_w = heads * dk
    c0 = 3 * conv_w
    ck = c0 + qk_w
    cv = c0 + 2 * qk_w
    co = cv + heads * dv
    cg = co + heads * dv

    lane = lax.broadcasted_iota(jnp.int32, (seg, LANES), 1)
    low_half = lane < group_dim
    for s in range(n_slots):
        r0 = s * seg
        u = p_ref[r0:r0 + seg, conv_w:2 * conv_w] * p_ref[r0:r0 + seg, 2 * conv_w:3 * conv_w]
        ubuf[s, 8:8 + seg, :] = u
        u1 = ubuf[s, 7:7 + seg, :]
        u2 = ubuf[s, 6:6 + seg, :]
        y = convb_ref[...] + convw_ref[0:1, :] * u2 + convw_ref[1:2, :] * u1 + convw_ref[2:3, :] * u
        ubuf[s, 6:8, :] = ubuf[s, seg + 6:seg + 8, :]
        a = p_ref[r0:r0 + seg, 0:conv_w] * y
        for jb in range(conv_w // LANES):
            blk = a[:, jb * LANES:(jb + 1) * LANES]
            sq = blk * blk
            lo = jnp.sum(jnp.where(low_half, sq, 0.0), axis=-1, keepdims=True)
            hi = jnp.sum(jnp.where(low_half, 0.0, sq), axis=-1, keepdims=True)
            msq = jnp.where(low_half, lo, hi) * (1.0 / group_dim)
            nb = blk * lax.rsqrt(msq + EPS) * gconv_ref[:, jb * LANES:(jb + 1) * LANES]
            mix_ref[r0:r0 + seg, jb * LANES:(jb + 1) * LANES] = nb.astype(_BF16)

    gates = p_ref[:, cg:cg + LANES]
    lf = _log_sigmoid(gates)
    rowc = lax.broadcasted_iota(jnp.int32, (tm, LANES), 0) % chunk
    bc = lf
    sh = 1
    while sh < chunk:
        bc = bc + jnp.where(rowc >= sh, pltpu.roll(bc, sh, axis=0), 0.0)
        sh *= 2
    c_all = gates - pltpu.roll(bc, LANES - heads, axis=1)

    row = lax.broadcasted_iota(jnp.int32, (chunk, lc), 0)
    col = lax.broadcasted_iota(jnp.int32, (chunk, lc), 1)
    causal_t = row <= col
    lane_c = lax.broadcasted_iota(jnp.int32, (lc, LANES), 1)
    head_lo = lane_c < dk
    lane_m = lax.broadcasted_iota(jnp.int32, (1, LANES), 1)
    real_t = lax.broadcasted_iota(jnp.int32, (1, lc), 1) < chunk
    ones_rows = jnp.where(lax.broadcasted_iota(jnp.int32, (dv, lc), 0) == 0, 1.0, 0.0)

    for s in range(n_slots):
        for j in range(n_sub):
            r0 = s * seg + j * chunk
            g_t = _pad_rows(gates[r0:r0 + chunk], lc).T
            b_t = _pad_rows(bc[r0:r0 + chunk], lc).T
            c_j = c_all[r0:r0 + chunk]
            m_all = m_scr[s]
            m_new_row = m_all
            for pr in range(pairs):
                q_pad = _pad_rows(p_ref[r0:r0 + chunk, c0 + pr * LANES:c0 + (pr + 1) * LANES], lc)
                k_pair = p_ref[r0:r0 + chunk, ck + pr * LANES:ck + (pr + 1) * LANES]
                k_bf = k_pair.astype(_BF16)
                k_pad = _pad_rows(k_pair, lc)
                cn_pair = cn_scr[s, pr]
                cn_bf = cn_pair.astype(_BF16)
                upd = None
                decays = []
                for hh in range(2):
                    h = 2 * pr + hh
                    hmask = head_lo if hh == 0 else jnp.logical_not(head_lo)
                    b_row = b_t[heads + h:heads + h + 1, :]
                    li_row = g_t[h:h + 1, :]
                    c_col = c_j[:, h:h + 1]
                    m_prev = m_all[:, h:h + 1]
                    d = jnp.where(causal_t, b_row + c_col, -jnp.inf)
                    inter = b_row + m_prev
                    m_t = jnp.maximum(inter, jnp.max(d, axis=0, keepdims=True))
                    w_intra = jnp.exp(d - m_t)
                    w_inter = jnp.exp(inter - m_t)
                    qm = jnp.where(hmask, q_pad, 0.0).astype(_BF16)
                    sc = _dot_nt(k_bf, qm) * w_intra
                    v_h = p_ref[r0:r0 + chunk, cv + h * dv:cv + (h + 1) * dv]
                    v_t = _pad_rows(v_h, lc).T
                    inter_cn = _dot_nt(cn_bf, qm)
                    num = _dot(v_t.astype(_BF16), _pad_rows(sc, lc).astype(_BF16)) + w_inter * inter_cn[0:dv]
                    den = jnp.sum(sc, axis=0, keepdims=True) + w_inter * inter_cn[dv:dv + 1]
                    h_t = num * (1.0 / jnp.maximum(jnp.abs(den), jnp.exp(-m_t)))
                    msq = jnp.mean(h_t * h_t, axis=0, keepdims=True)
                    h_n = (h_t * lax.rsqrt(msq + EPS)).T[0:chunk]
                    o_h = p_ref[r0:r0 + chunk, co + h * dv:co + (h + 1) * dv]
                    hm = h_n * gml_ref[:, h * dv:(h + 1) * dv] * jax.nn.sigmoid(o_h)
                    mix_ref[r0:r0 + chunk, conv_w + h * dv:conv_w + (h + 1) * dv] = hm.astype(_BF16)
                    b_last = b_row[:, chunk - 1:chunk]
                    m_new = m_t[:, chunk - 1:chunk]
                    wk = jnp.where(real_t, jnp.exp(b_last - b_row + li_row - m_new), 0.0)
                    decays.append(jnp.exp(b_last + m_prev - m_new))
                    vw = (jnp.concatenate([v_t, ones_rows], axis=0) * wk).astype(_BF16)
                    contrib = _dot(vw, jnp.where(hmask, k_pad, 0.0).astype(_BF16))
                    upd = contrib if upd is None else upd + contrib
                    m_new_row = jnp.where(lane_m == h, m_new, m_new_row)
                decay_lanes = jnp.where(lane_m < dk, decays[0], decays[1])
                cn_scr[s, pr] = decay_lanes * cn_pair + upd
            m_scr[s] = m_new_row


def _mixer_kernel(xa_ref, xc_ref, conv0_ref, cn0_ref, m0_ref, nrm_ref, win_ref, gbias_ref, convw_ref,
                  convb_ref, gconv_ref, gml_ref, wout_ref,
                  xo_ref, convo_ref, cno_ref, mo_ref,
                  p0, p1, mix0, mix1, ubuf, cn_scr, m_scr,
                  *, n_tiles, n_t, tm, chunk, n_slots, heads, dk, dv, conv_w, group_dim):
    s = pl.program_id(0)
    tile_b = jnp.clip(s - 1, 0, n_tiles - 1)
    t_b = tile_b % n_t
    valid_b = jnp.logical_and(s >= 1, s <= n_tiles)

    @pl.when(s == 0)
    def _zero():
        p1[...] = jnp.zeros_like(p1)
        mix0[...] = jnp.zeros_like(mix0)
        ubuf[...] = jnp.zeros_like(ubuf)
        cn_scr[...] = jnp.zeros_like(cn_scr)
        m_scr[...] = jnp.zeros_like(m_scr)

    @pl.when(jnp.logical_and(valid_b, t_b == 0))
    def _init():
        for i in range(n_slots):
            ubuf[i, 6:8, :] = conv0_ref[i]
        cn_scr[...] = cn0_ref[...]
        m_scr[...] = m0_ref[...]

    def step(p_w, p_r, mix_w, mix_r):
        _project(xa_ref, nrm_ref, win_ref, gbias_ref, p_w, heads=heads, dk=dk, dv=dv, conv_w=conv_w)
        _recurrent(p_r, mix_w, convw_ref, convb_ref, gconv_ref, gml_ref, ubuf, cn_scr, m_scr,
                   tm=tm, chunk=chunk, n_slots=n_slots, heads=heads, dk=dk, dv=dv, conv_w=conv_w,
                   group_dim=group_dim)
        xo_ref[...] = xc_ref[...] + _dot(mix_r[...], wout_ref[...])

    @pl.when(s % 2 == 0)
    def _even():
        step(p0, p1, mix1, mix0)

    @pl.when(s % 2 == 1)
    def _odd():
        step(p1, p0, mix0, mix1)

    @pl.when(jnp.logical_and(valid_b, t_b == n_t - 1))
    def _fin():
        for i in range(n_slots):
            convo_ref[i] = ubuf[i, 6:8, :]
        cno_ref[...] = cn_scr[...]
        mo_ref[...] = m_scr[...]


def _mixer(x, conv0, cn0, m0, nrm, win, gbias, convw, convb, gconv, gml, wout,
           *, layer, n_seq, seq_len, tm, chunk, n_slots, heads, dk, dv, conv_w, group_dim):
    n, d = x.shape
    seg = tm // n_slots
    n_t = seq_len // seg
    n_tiles = n // tm
    pairs = heads // 2
    nin = win.shape[2]
    kern = functools.partial(_mixer_kernel, n_tiles=n_tiles, n_t=n_t, tm=tm, chunk=chunk,
                             n_slots=n_slots, heads=heads, dk=dk, dv=dv, conv_w=conv_w,
                             group_dim=group_dim)
    tile_a = lambda s: (jnp.minimum(s, n_tiles - 1), 0)
    tile_c = lambda s: (jnp.maximum(s - 2, 0), 0)
    seq_b = lambda s: jnp.clip(s - 1, 0, n_tiles - 1) // n_t
    state3 = lambda s: (layer, seq_b(s), 0, 0)
    state4 = lambda s: (layer, seq_b(s), 0, 0, 0)
    out3 = lambda s: (seq_b(s), 0, 0)
    out4 = lambda s: (seq_b(s), 0, 0, 0)
    return pl.pallas_call(
        kern,
        out_shape=(
            jax.ShapeDtypeStruct((n, d), _F32),
            jax.ShapeDtypeStruct((n_seq, 2, conv_w), _F32),
            jax.ShapeDtypeStruct((n_seq, pairs, 2 * dv, 2 * dk), _F32),
            jax.ShapeDtypeStruct((n_seq, 1, LANES), _F32),
        ),
        grid=(n_tiles + 2,),
        in_specs=[
            pl.BlockSpec((tm, d), tile_a),
            pl.BlockSpec((tm, d), tile_c),
            pl.BlockSpec((None, n_slots, 2, conv_w), state3),
            pl.BlockSpec((None, n_slots, pairs, 2 * dv, 2 * dk), state4),
            pl.BlockSpec((None, n_slots, 1, LANES), state3),
            _layer_spec((1, d), layer),
            _layer_spec((d, nin), layer),
            _layer_spec((1, LANES), layer),
            _layer_spec((3, conv_w), layer),
            _layer_spec((1, conv_w), layer),
            _layer_spec((1, conv_w), layer),
            _layer_spec((1, heads * dv), layer),
            _layer_spec((conv_w + heads * dv, d), layer),
        ],
        out_specs=(
            pl.BlockSpec((tm, d), tile_c),
            pl.BlockSpec((n_slots, 2, conv_w), out3),
            pl.BlockSpec((n_slots, pairs, 2 * dv, 2 * dk), out4),
            pl.BlockSpec((n_slots, 1, LANES), out3),
        ),
        scratch_shapes=[
            pltpu.VMEM((tm, nin), _F32),
            pltpu.VMEM((tm, nin), _F32),
            pltpu.VMEM((tm, conv_w + heads * dv), _BF16),
            pltpu.VMEM((tm, conv_w + heads * dv), _BF16),
            pltpu.VMEM((n_slots, seg + 8, conv_w), _F32),
            pltpu.VMEM((n_slots, pairs, 2 * dv, 2 * dk), _F32),
            pltpu.VMEM((n_slots, 1, LANES), _F32),
        ],
        compiler_params=pltpu.CompilerParams(
            dimension_semantics=("arbitrary",), vmem_limit_bytes=VMEM_LIMIT_BYTES),
        name="mixer",
    )(x, x, conv0, cn0, m0, nrm, win, gbias, convw, convb, gconv, gml, wout)


def _prep_weights(ffn1_norm, ffn1_w_gate, ffn1_w_up, ffn1_w_down, mix_norm, w_in, b_igate, b_fgate,
                  conv_w, conv_b, g_conv, g_mlstm, w_out, ffn2_norm, ffn2_w_gate, ffn2_w_up,
                  ffn2_w_down, final_norm):
    depth, d, _ = w_in.shape
    heads = b_igate.shape[1]
    n_main = w_in.shape[2] - 2 * heads
    w_in_p = jnp.concatenate(
        [w_in[:, :, :n_main].astype(_BF16), w_in[:, :, n_main:].astype(_BF16),
         jnp.zeros((depth, d, LANES - 2 * heads), _BF16)], axis=-1)
    gbias = jnp.concatenate(
        [b_igate, b_fgate, jnp.zeros((depth, LANES - 2 * heads), _F32)], axis=-1)[:, None, :]
    return dict(
        ffn1_norm=ffn1_norm[:, None, :], ffn1_w_gate=ffn1_w_gate.astype(_BF16),
        ffn1_w_up=ffn1_w_up.astype(_BF16), ffn1_w_down=ffn1_w_down.astype(_BF16),
        mix_norm=mix_norm[:, None, :], w_in=w_in_p, gbias=gbias, conv_w=conv_w,
        conv_b=conv_b[:, None, :], g_conv=g_conv.reshape(depth, 1, -1),
        g_mlstm=g_mlstm.reshape(depth, 1, -1), w_out=w_out.astype(_BF16),
        ffn2_norm=ffn2_norm[:, None, :], ffn2_w_gate=ffn2_w_gate.astype(_BF16),
        ffn2_w_up=ffn2_w_up.astype(_BF16), ffn2_w_down=ffn2_w_down.astype(_BF16),
        final_norm=final_norm[None, :],
    )


def _tile_plan(n_seq, seq_len):
    if seq_len >= 256:
        chunk = 256
        tm = 512 if seq_len % 512 == 0 else 256
        return tm, chunk, 1
    chunk = seq_len
    n_slots = max(1, min(n_seq, 256 // seq_len))
    while n_seq % n_slots:
        n_slots -= 1
    return n_slots * seq_len, chunk, n_slots


def _trunk(x, conv0, c0, n0, m0, w, dims):
    n_seq, seq_len, d = x.shape
    depth = w["w_in"].shape[0]
    heads, dk, dv, conv_w, group_dim = dims
    pairs = heads // 2
    tm, chunk, n_slots = _tile_plan(n_seq, seq_len)
    n_rows = n_seq * seq_len
    tm_ffn, ff_split = (1024, 3) if n_rows % 1024 == 0 else (tm, 1)
    xf = x.reshape(n_rows, d)
    cn0 = jnp.concatenate(
        [jnp.swapaxes(c0, -1, -2), n0[..., None, :], jnp.zeros(c0.shape[:-2] + (dv - 1, dk), _F32)], axis=-2)
    cn0 = jnp.swapaxes(cn0.reshape(depth, n_seq, pairs, 2, 2 * dv, dk), 3, 4)
    cn0 = cn0.reshape(depth, n_seq, pairs, 2 * dv, 2 * dk)
    m0p = jnp.concatenate([m0, jnp.zeros((depth, n_seq, LANES - heads), _F32)], axis=-1)[:, :, None, :]
    convs, cns, ms = [], [], []
    for l in range(depth):
        xf = _ffn(xf, w["ffn1_norm"], w["ffn1_w_gate"], w["ffn1_w_up"], w["ffn1_w_down"],
                  w["final_norm"], layer=l, tm=tm_ffn, ff_split=ff_split, final_norm=False)
        xf, cv, cn, m = _mixer(
            xf, conv0, cn0, m0p, w["mix_norm"], w["w_in"], w["gbias"], w["conv_w"],
            w["conv_b"], w["g_conv"], w["g_mlstm"], w["w_out"],
            layer=l, n_seq=n_seq, seq_len=seq_len, tm=tm, chunk=chunk, n_slots=n_slots, heads=heads,
            dk=dk, dv=dv, conv_w=conv_w, group_dim=group_dim)
        xf = _ffn(xf, w["ffn2_norm"], w["ffn2_w_gate"], w["ffn2_w_up"], w["ffn2_w_down"],
                  w["final_norm"], layer=l, tm=tm_ffn, ff_split=ff_split, final_norm=(l == depth - 1))
        convs.append(cv)
        cns.append(cn)
        ms.append(m)
    cn = jnp.stack(cns).reshape(depth, n_seq, pairs, 2 * dv, 2, dk)
    cn = jnp.swapaxes(cn, 3, 4).reshape(depth, n_seq, heads, 2 * dv, dk)
    return (xf.reshape(n_seq, seq_len, d), jnp.stack(convs), jnp.swapaxes(cn[..., :dv, :], -1, -2),
            cn[..., dv, :], jnp.stack(ms)[:, :, 0, :heads])


def kernel(x_prompt, x_sample, cache_conv, state_C, state_n, state_m, ffn1_norm, ffn1_w_gate, ffn1_w_up, ffn1_w_down, mix_norm, w_in, b_igate, b_fgate, conv_w, conv_b, g_conv, g_mlstm, w_out, ffn2_norm, ffn2_w_gate, ffn2_w_up, ffn2_w_down, final_norm):
    depth, _, heads, dk, dv = state_C.shape
    cw = conv_w.shape[2]
    group_dim = g_conv.shape[2]
    assert heads % 2 == 0 and 2 * dk == LANES and dv == LANES and 2 * group_dim == LANES
    dims = (heads, dk, dv, cw, group_dim)
    w = _prep_weights(ffn1_norm, ffn1_w_gate, ffn1_w_up, ffn1_w_down, mix_norm, w_in, b_igate, b_fgate,
                      conv_w, conv_b, g_conv, g_mlstm, w_out, ffn2_norm, ffn2_w_gate, ffn2_w_up,
                      ffn2_w_down, final_norm)
    bsz = x_prompt.shape[0]
    zc = jnp.zeros((depth, bsz, 2, cw), _F32)
    zC = jnp.zeros((depth, bsz, heads, dk, dv), _F32)
    zn = jnp.zeros((depth, bsz, heads, dk), _F32)
    zm = jnp.zeros((depth, bsz, heads), _F32)
    y_p, p_conv, p_C, p_n, p_m = _trunk(x_prompt, zc, zC, zn, zm, w, dims)
    y_s, s_conv, s_C, s_n, s_m = _trunk(x_sample, cache_conv, state_C, state_n, state_m, w, dims)
    return (y_p, y_s, p_conv, p_C, p_n, p_m, s_conv, s_C, s_n, s_m)
```

```python
import functools

import jax
import jax.numpy as jnp
from jax import lax
from jax.experimental import pallas as pl
from jax.experimental.pallas import tpu as pltpu

EPS = 1e-6
LANES = 128
MXU_WIDTH = 256
BF16_ROWS = 16
VMEM_LIMIT_BYTES = 56 * 1024 * 1024

_BF16 = jnp.bfloat16
_F32 = jnp.float32


def _dot(a, b):
    return jnp.dot(a, b, preferred_element_type=_F32)


def _dot_nt(a, b):
    return lax.dot_general(a, b, (((1,), (1,)), ((), ())), preferred_element_type=_F32)


def _layer_spec(shape, layer):
    return pl.BlockSpec((None,) + shape, lambda *_: (layer, 0, 0), pipeline_mode=pl.Buffered(1))


def _ffn_kernel(xp_ref, xs_ref, g_ref, wg_ref, wu_ref, wd_ref, fg_ref, op_ref, os_ref,
                *, n_p, final_norm, ff_chunks):
    def run(x_ref, o_ref):
        x = x_ref[...]
        ms = jnp.mean(x * x, axis=-1, keepdims=True)
        hn = (x * lax.rsqrt(ms + EPS) * g_ref[...]).astype(_BF16)
        y = None
        for lo, hi in ff_chunks:
            gate = _dot(hn, wg_ref[:, lo:hi])
            up = _dot(hn, wu_ref[:, lo:hi])
            act = (gate * jax.nn.sigmoid(gate) * up).astype(_BF16)
            part = _dot(act, wd_ref[lo:hi, :])
            y = part if y is None else y + part
        y = x + 0.5 * y
        if final_norm:
            ms2 = jnp.mean(y * y, axis=-1, keepdims=True)
            y = y * lax.rsqrt(ms2 + EPS) * fg_ref[...]
        o_ref[...] = y

    i = pl.program_id(0)

    @pl.when(i < n_p)
    def _prompt():
        run(xp_ref, op_ref)

    @pl.when(i >= n_p)
    def _sample():
        run(xs_ref, os_ref)


def _ff_chunks(dff, n_chunks):
    tiles = -(-dff // MXU_WIDTH)
    bounds = [min(dff, MXU_WIDTH * ((tiles * i) // n_chunks)) for i in range(n_chunks + 1)]
    return tuple((bounds[i], bounds[i + 1]) for i in range(n_chunks) if bounds[i + 1] > bounds[i])


def _ffn(xp, xs, g, wg, wu, wd, fg, *, layer, tm_p, tm_s, ff_split, final_norm):
    d = xp.shape[1]
    dff = wg.shape[2]
    n_p, n_s = xp.shape[0] // tm_p, xs.shape[0] // tm_s
    prompt_tile = lambda i: (jnp.minimum(i, n_p - 1), 0)
    sample_tile = lambda i: (jnp.maximum(i - n_p, 0), 0)
    return pl.pallas_call(
        functools.partial(_ffn_kernel, n_p=n_p, final_norm=final_norm, ff_chunks=_ff_chunks(dff, ff_split)),
        out_shape=(jax.ShapeDtypeStruct(xp.shape, _F32), jax.ShapeDtypeStruct(xs.shape, _F32)),
        grid=(n_p + n_s,),
        in_specs=[
            pl.BlockSpec((tm_p, d), prompt_tile),
            pl.BlockSpec((tm_s, d), sample_tile),
            _layer_spec((1, d), layer),
            _layer_spec((d, dff), layer),
            _layer_spec((d, dff), layer),
            _layer_spec((dff, d), layer),
            pl.BlockSpec((1, d), lambda i: (0, 0)),
        ],
        out_specs=(pl.BlockSpec((tm_p, d), prompt_tile), pl.BlockSpec((tm_s, d), sample_tile)),
        compiler_params=pltpu.CompilerParams(
            dimension_semantics=("arbitrary",), vmem_limit_bytes=VMEM_LIMIT_BYTES),
        name="ffn_final" if final_norm else "ffn",
    )(xp, xs, g, wg, wu, wd, fg)


def _log_sigmoid(x):
    return jnp.minimum(x, 0.0) - jnp.log1p(jnp.exp(-jnp.abs(x)))


def _pad_rows(a, rows):
    if a.shape[0] == rows:
        return a
    return jnp.concatenate([a, jnp.zeros((rows - a.shape[0], a.shape[1]), a.dtype)], axis=0)


def _project(xa_ref, nrm_ref, win_ref, wgate_ref, gbias_ref, p_ref, *, heads, dk, dv, conv_w):
    x = xa_ref[...]
    ms = jnp.mean(x * x, axis=-1, keepdims=True)
    hn = (x * lax.rsqrt(ms + EPS) * nrm_ref[...]).astype(_BF16)
    qk_w = heads * dk
    c0 = 3 * conv_w
    c1 = c0 + 2 * qk_w
    c2 = c1 + 2 * heads * dv
    p_ref[:, 0:c0] = _dot(hn, win_ref[:, 0:c0])
    p_ref[:, c0:c0 + qk_w] = _dot(hn, win_ref[:, c0:c0 + qk_w]) * (dk ** -0.5)
    p_ref[:, c0 + qk_w:c1] = _dot(hn, win_ref[:, c0 + qk_w:c1])
    p_ref[:, c1:c2] = _dot(hn, win_ref[:, c1:c2])
    p_ref[:, c2:c2 + LANES] = _dot(hn, wgate_ref[...]) + gbias_ref[...]


def _recurrent(p_ref, mix_ref, convw_ref, convb_ref, gconv_ref, gml_ref, ubuf, cn_scr, m_scr,
               *, tm, chunk, n_slots, heads, dk, dv, conv_w, group_dim):
    seg = tm // n_slots
    n_sub = seg // chunk
    lc = max(chunk, LANES)
    pairs = heads // 2
    qk_w = heads * dk
    c0 = 3 * conv_w
    ck = c0 + qk_w
    cv = c0 + 2 * qk_w
    co = cv + heads * dv
    cg = co + heads * dv

    lane = lax.broadcasted_iota(jnp.int32, (seg, LANES), 1)
    low_half = lane < group_dim
    for s in range(n_slots):
        r0 = s * seg
        u = p_ref[r0:r0 + seg, conv_w:2 * conv_w] * p_ref[r0:r0 + seg, 2 * conv_w:3 * conv_w]
        ubuf[s, 8:8 + seg, :] = u
        u1 = ubuf[s, 7:7 + seg, :]
        u2 = ubuf[s, 6:6 + seg, :]
        y = convb_ref[...] + convw_ref[0:1, :] * u2 + convw_ref[1:2, :] * u1 + convw_ref[2:3, :] * u
        ubuf[s, 6:8, :] = ubuf[s, seg + 6:seg + 8, :]
        a = p_ref[r0:r0 + seg, 0:conv_w] * y
        for jb in range(conv_w // LANES):
            blk = a[:, jb * LANES:(jb + 1) * LANES]
            sq = blk * blk
            lo = jnp.sum(jnp.where(low_half, sq, 0.0), axis=-1, keepdims=True)
            hi = jnp.sum(jnp.where(low_half, 0.0, sq), axis=-1, keepdims=True)
            msq = jnp.where(low_half, lo, hi) * (1.0 / group_dim)
            nb = blk * lax.rsqrt(msq + EPS) * gconv_ref[:, jb * LANES:(jb + 1) * LANES]
            mix_ref[r0:r0 + seg, jb * LANES:(jb + 1) * LANES] = nb.astype(_BF16)

    gates = p_ref[:, cg:cg + LANES]
    lf = _log_sigmoid(gates)
    rowc = lax.broadcasted_iota(jnp.int32, (tm, LANES), 0) % chunk
    bc = lf
    sh = 1
    while sh < chunk:
        bc = bc + jnp.where(rowc >= sh, pltpu.roll(bc, sh, axis=0), 0.0)
        sh *= 2
    c_all = gates - pltpu.roll(bc, LANES - heads, axis=1)

    row = lax.broadcasted_iota(jnp.int32, (chunk, lc), 0)
    col = lax.broadcasted_iota(jnp.int32, (chunk, lc), 1)
    causal_t = row <= col
    lane_c = lax.broadcasted_iota(jnp.int32, (lc, LANES), 1)
    head_lo = lane_c < dk
    lane_m = lax.broadcasted_iota(jnp.int32, (1, LANES), 1)
    real_t = lax.broadcasted_iota(jnp.int32, (1, lc), 1) < chunk
    ones_rows = jnp.where(lax.broadcasted_iota(jnp.int32, (BF16_ROWS, lc), 0) == 0, 1.0, 0.0)

    for s in range(n_slots):
        for j in range(n_sub):
            r0 = s * seg + j * chunk
            g_t = _pad_rows(gates[r0:r0 + chunk], lc).T
            b_t = _pad_rows(bc[r0:r0 + chunk], lc).T
            c_j = c_all[r0:r0 + chunk]
            m_all = m_scr[s]
            m_new_row = m_all
            for pr in range(pairs):
                q_pad = _pad_rows(p_ref[r0:r0 + chunk, c0 + pr * LANES:c0 + (pr + 1) * LANES], lc)
                k_pair = p_ref[r0:r0 + chunk, ck + pr * LANES:ck + (pr + 1) * LANES]
                k_bf = k_pair.astype(_BF16)
                k_pad = _pad_rows(k_pair, lc)
                cn_pair = cn_scr[s, pr]
                cn_bf = cn_pair.astype(_BF16)
                upd = None
                decays = []
                for hh in range(2):
                    h = 2 * pr + hh
                    hmask = head_lo if hh == 0 else jnp.logical_not(head_lo)
                    b_row = b_t[heads + h:heads + h + 1, :]
                    li_row = g_t[h:h + 1, :]
                    c_col = c_j[:, h:h + 1]
                    m_prev = m_all[:, h:h + 1]
                    d = jnp.where(causal_t, b_row + c_col, -jnp.inf)
                    inter = b_row + m_prev
                    m_t = jnp.maximum(inter, jnp.max(d, axis=0, keepdims=True))
                    w_intra = jnp.exp(d - m_t)
                    w_inter = jnp.exp(inter - m_t)
                    qm = jnp.where(hmask, q_pad, 0.0).astype(_BF16)
                    sc = _dot_nt(k_bf, qm) * w_intra
                    v_h = p_ref[r0:r0 + chunk, cv + h * dv:cv + (h + 1) * dv]
                    v_t = _pad_rows(v_h, lc).T
                    inter_cn = _dot_nt(cn_bf, qm)
                    num = _dot(v_t.astype(_BF16), _pad_rows(sc, lc).astype(_BF16)) + w_inter * inter_cn[0:dv]
                    den = jnp.sum(sc, axis=0, keepdims=True) + w_inter * inter_cn[dv:dv + 1]
                    h_t = num * (1.0 / jnp.maximum(jnp.abs(den), jnp.exp(-m_t)))
                    msq = jnp.mean(h_t * h_t, axis=0, keepdims=True)
                    h_n = (h_t * lax.rsqrt(msq + EPS)).T[0:chunk]
                    o_h = p_ref[r0:r0 + chunk, co + h * dv:co + (h + 1) * dv]
                    hm = h_n * gml_ref[:, h * dv:(h + 1) * dv] * jax.nn.sigmoid(o_h)
                    mix_ref[r0:r0 + chunk, conv_w + h * dv:conv_w + (h + 1) * dv] = hm.astype(_BF16)
                    b_last = b_row[:, chunk - 1:chunk]
                    m_new = m_t[:, chunk - 1:chunk]
                    wk = jnp.where(real_t, jnp.exp(b_last - b_row + li_row - m_new), 0.0)
                    decays.append(jnp.exp(b_last + m_prev - m_new))
                    vw = (jnp.concatenate([v_t, ones_rows], axis=0) * wk).astype(_BF16)
                    contrib = _dot(vw, jnp.where(hmask, k_pad, 0.0).astype(_BF16))
                    upd = contrib if upd is None else upd + contrib
                    m_new_row = jnp.where(lane_m == h, m_new, m_new_row)
                decay_lanes = jnp.where(lane_m < dk, decays[0], decays[1])
                cn_scr[s, pr] = decay_lanes * cn_pair + upd
            m_scr[s] = m_new_row


def _mixer_kernel(xa_ref, xc_ref, conv0_ref, c0_ref, n0_ref, m0_ref, nrm_ref, win_ref, wgate_ref, gbias_ref, convw_ref,
                  convb_ref, gconv_ref, gml_ref, wout_ref, conv_acc, c_acc, n_acc, m_acc,
                  xo_ref, convo_ref, co_ref, no_ref, mo_ref,
                  p0, p1, mix0, mix1, ubuf, cn_scr, m_scr,
                  *, n_tiles, n_t, tm, chunk, n_slots, heads, dk, dv, conv_w, group_dim):
    s = pl.program_id(0)
    tile_b = jnp.clip(s - 1, 0, n_tiles - 1)
    t_b = tile_b % n_t
    valid_b = jnp.logical_and(s >= 1, s <= n_tiles)

    @pl.when(s == 0)
    def _zero():
        p1[...] = jnp.zeros_like(p1)
        mix0[...] = jnp.zeros_like(mix0)
        ubuf[...] = jnp.zeros_like(ubuf)
        cn_scr[...] = jnp.zeros_like(cn_scr)
        m_scr[...] = jnp.zeros_like(m_scr)

    pairs = heads // 2

    @pl.when(jnp.logical_and(valid_b, t_b == 0))
    def _init():
        for i in range(n_slots):
            ubuf[i, 6:8, :] = conv0_ref[i]
            for pr in range(pairs):
                c_pair = c0_ref[i, 2 * pr:2 * pr + 2].reshape(2 * dk, dv)
                cn_scr[i, pr, 0:dv, :] = c_pair.T
                cn_scr[i, pr, dv:dv + BF16_ROWS, :] = _pad_rows(n0_ref[i, pr], BF16_ROWS)
        m_scr[...] = m0_ref[...]

    def step(p_w, p_r, mix_w, mix_r):
        _project(xa_ref, nrm_ref, win_ref, wgate_ref, gbias_ref, p_w, heads=heads, dk=dk, dv=dv, conv_w=conv_w)
        _recurrent(p_r, mix_w, convw_ref, convb_ref, gconv_ref, gml_ref, ubuf, cn_scr, m_scr,
                   tm=tm, chunk=chunk, n_slots=n_slots, heads=heads, dk=dk, dv=dv, conv_w=conv_w,
                   group_dim=group_dim)
        xo_ref[...] = xc_ref[...] + _dot(mix_r[...], wout_ref[...])

    @pl.when(s % 2 == 0)
    def _even():
        step(p0, p1, mix1, mix0)

    @pl.when(s % 2 == 1)
    def _odd():
        step(p1, p0, mix0, mix1)

    @pl.when(jnp.logical_and(valid_b, t_b == n_t - 1))
    def _fin():
        for i in range(n_slots):
            convo_ref[i] = ubuf[i, 6:8, :]
            for pr in range(pairs):
                co_ref[i, 2 * pr:2 * pr + 2] = cn_scr[i, pr, 0:dv, :].T.reshape(2, dk, dv)
                no_ref[i, pr] = cn_scr[i, pr, dv:dv + 1, :]
        mo_ref[...] = m_scr[...]


def _mixer(x, conv0, c0, n0, m0, nrm, win, wgate, gbias, convw, convb, gconv, gml, wout, acc,
           *, layer, n_seq, seq_len, tm, chunk, n_slots, heads, dk, dv, conv_w, group_dim):
    n, d = x.shape
    seg = tm // n_slots
    n_t = seq_len // seg
    n_tiles = n // tm
    pairs = heads // 2
    n_main = win.shape[2]
    nin = n_main + LANES
    kern = functools.partial(_mixer_kernel, n_tiles=n_tiles, n_t=n_t, tm=tm, chunk=chunk,
                             n_slots=n_slots, heads=heads, dk=dk, dv=dv, conv_w=conv_w,
                             group_dim=group_dim)
    tile_a = lambda s: (jnp.minimum(s, n_tiles - 1), 0)
    tile_c = lambda s: (jnp.maximum(s - 2, 0), 0)
    seq_b = lambda s: jnp.clip(s - 1, 0, n_tiles - 1) // n_t
    state_specs = [
        pl.BlockSpec((None, n_slots, 2, conv_w), lambda s: (layer, seq_b(s), 0, 0)),
        pl.BlockSpec((None, n_slots, heads, dk, dv), lambda s: (layer, seq_b(s), 0, 0, 0)),
        pl.BlockSpec((None, n_slots, pairs, 1, 2 * dk), lambda s: (layer, seq_b(s), 0, 0, 0)),
        pl.BlockSpec((None, n_slots, 1, LANES), lambda s: (layer, seq_b(s), 0, 0)),
    ]
    operands = (x, x, conv0, c0, n0, m0, nrm, win, wgate, gbias, convw, convb, gconv, gml, wout)
    return pl.pallas_call(
        kern,
        out_shape=(jax.ShapeDtypeStruct((n, d), _F32),) + tuple(
            jax.ShapeDtypeStruct(a.shape, a.dtype) for a in acc),
        grid=(n_tiles + 2,),
        in_specs=[
            pl.BlockSpec((tm, d), tile_a),
            pl.BlockSpec((tm, d), tile_c),
            *state_specs,
            _layer_spec((1, d), layer),
            _layer_spec((d, n_main), layer),
            _layer_spec((d, LANES), layer),
            _layer_spec((1, LANES), layer),
            _layer_spec((3, conv_w), layer),
            _layer_spec((1, conv_w), layer),
            _layer_spec((1, conv_w), layer),
            _layer_spec((1, heads * dv), layer),
            _layer_spec((conv_w + heads * dv, d), layer),
            *[pl.BlockSpec(memory_space=pl.ANY)] * len(acc),
        ],
        out_specs=(pl.BlockSpec((tm, d), tile_c), *state_specs),
        input_output_aliases={len(operands) + i: 1 + i for i in range(len(acc))},
        scratch_shapes=[
            pltpu.VMEM((tm, nin), _F32),
            pltpu.VMEM((tm, nin), _F32),
            pltpu.VMEM((tm, conv_w + heads * dv), _BF16),
            pltpu.VMEM((tm, conv_w + heads * dv), _BF16),
            pltpu.VMEM((n_slots, seg + 8, conv_w), _F32),
            pltpu.VMEM((n_slots, pairs, dv + BF16_ROWS, 2 * dk), _F32),
            pltpu.VMEM((n_slots, 1, LANES), _F32),
        ],
        compiler_params=pltpu.CompilerParams(
            dimension_semantics=("arbitrary",), vmem_limit_bytes=VMEM_LIMIT_BYTES),
        name="mixer",
    )(*operands, *acc)


def _prep_weights(ffn1_norm, ffn1_w_gate, ffn1_w_up, ffn1_w_down, mix_norm, w_in, b_igate, b_fgate,
                  conv_w, conv_b, g_conv, g_mlstm, w_out, ffn2_norm, ffn2_w_gate, ffn2_w_up,
                  ffn2_w_down, final_norm):
    depth, d, _ = w_in.shape
    heads = b_igate.shape[1]
    n_main = w_in.shape[2] - 2 * heads
    w_main = w_in[:, :, :n_main].astype(_BF16)
    w_gate = jnp.concatenate(
        [w_in[:, :, n_main:].astype(_BF16), jnp.zeros((depth, d, LANES - 2 * heads), _BF16)], axis=-1)
    gbias = jnp.concatenate(
        [b_igate, b_fgate, jnp.zeros((depth, LANES - 2 * heads), _F32)], axis=-1)[:, None, :]
    return dict(
        ffn1_norm=ffn1_norm[:, None, :], ffn1_w_gate=ffn1_w_gate.astype(_BF16),
        ffn1_w_up=ffn1_w_up.astype(_BF16), ffn1_w_down=ffn1_w_down.astype(_BF16),
        mix_norm=mix_norm[:, None, :], w_in=w_main, w_gate=w_gate, gbias=gbias, conv_w=conv_w,
        conv_b=conv_b[:, None, :], g_conv=g_conv.reshape(depth, 1, -1),
        g_mlstm=g_mlstm.reshape(depth, 1, -1), w_out=w_out.astype(_BF16),
        ffn2_norm=ffn2_norm[:, None, :], ffn2_w_gate=ffn2_w_gate.astype(_BF16),
        ffn2_w_up=ffn2_w_up.astype(_BF16), ffn2_w_down=ffn2_w_down.astype(_BF16),
        final_norm=final_norm[None, :],
    )


def _tile_plan(n_seq, seq_len):
    if seq_len >= 256:
        chunk = 256
        tm = 512 if seq_len % 512 == 0 else 256
        return tm, chunk, 1
    chunk = seq_len
    n_slots = max(1, min(n_seq, 256 // seq_len))
    while n_seq % n_slots:
        n_slots -= 1
    return n_slots * seq_len, chunk, n_slots


def _stream_state(conv0, c0, n0, m0, dims):
    heads, dk = dims[0], dims[1]
    depth, n_seq = m0.shape[0], m0.shape[1]
    n0r = n0.reshape(depth, n_seq, heads // 2, 1, 2 * dk)
    m0p = jnp.concatenate([m0, jnp.zeros((depth, n_seq, LANES - heads), _F32)], axis=-1)[:, :, None, :]
    state = (conv0, c0, n0r, m0p)
    return state, tuple(jnp.zeros(a.shape, _F32) for a in state)


def _stream_outputs(acc, dims):
    heads, dk = dims[0], dims[1]
    conv_o, c_o, n_o, m_o = acc
    depth, n_seq = m_o.shape[0], m_o.shape[1]
    return conv_o, c_o, n_o.reshape(depth, n_seq, heads, dk), m_o[:, :, 0, :heads]


def kernel(x_prompt, x_sample, cache_conv, state_C, state_n, state_m, ffn1_norm, ffn1_w_gate, ffn1_w_up, ffn1_w_down, mix_norm, w_in, b_igate, b_fgate, conv_w, conv_b, g_conv, g_mlstm, w_out, ffn2_norm, ffn2_w_gate, ffn2_w_up, ffn2_w_down, final_norm):
    depth, _, heads, dk, dv = state_C.shape
    cw = conv_w.shape[2]
    group_dim = g_conv.shape[2]
    assert heads % 2 == 0 and 2 * dk == LANES and dv == LANES and 2 * group_dim == LANES
    dims = (heads, dk, dv, cw, group_dim)
    w = _prep_weights(ffn1_norm, ffn1_w_gate, ffn1_w_up, ffn1_w_down, mix_norm, w_in, b_igate, b_fgate,
                      conv_w, conv_b, g_conv, g_mlstm, w_out, ffn2_norm, ffn2_w_gate, ffn2_w_up,
                      ffn2_w_down, final_norm)
    bsz, seq, d = x_prompt.shape
    dbsz, dseq, _ = x_sample.shape
    p_state, p_acc = _stream_state(
        jnp.zeros((depth, bsz, 2, cw), _F32), jnp.zeros((depth, bsz, heads, dk, dv), _F32),
        jnp.zeros((depth, bsz, heads, dk), _F32), jnp.zeros((depth, bsz, heads), _F32), dims)
    s_state, s_acc = _stream_state(cache_conv, state_C, state_n, state_m, dims)
    p_plan = _tile_plan(bsz, seq)
    s_plan = _tile_plan(dbsz, dseq)
    xp = x_prompt.reshape(bsz * seq, d)
    xs = x_sample.reshape(dbsz * dseq, d)
    tm_p = 1024 if xp.shape[0] % 1024 == 0 else p_plan[0]
    tm_s = 256 if xs.shape[0] % 256 == 0 else s_plan[0]
    ff_split = 3 if tm_p >= 1024 else 1

    def mixer(x, state, acc, plan, n_seq, seq_len, l):
        tm, chunk, n_slots = plan
        x, *acc = _mixer(
            x, *state, w["mix_norm"], w["w_in"], w["w_gate"], w["gbias"], w["conv_w"], w["conv_b"],
            w["g_conv"], w["g_mlstm"], w["w_out"], acc,
            layer=l, n_seq=n_seq, seq_len=seq_len, tm=tm, chunk=chunk, n_slots=n_slots, heads=heads,
            dk=dk, dv=dv, conv_w=cw, group_dim=group_dim)
        return x, tuple(acc)

    for l in range(depth):
        xp, xs = _ffn(xp, xs, w["ffn1_norm"], w["ffn1_w_gate"], w["ffn1_w_up"], w["ffn1_w_down"],
                      w["final_norm"], layer=l, tm_p=tm_p, tm_s=tm_s, ff_split=ff_split, final_norm=False)
        xp, p_acc = mixer(xp, p_state, p_acc, p_plan, bsz, seq, l)
        xs, s_acc = mixer(xs, s_state, s_acc, s_plan, dbsz, dseq, l)
        xp, xs = _ffn(xp, xs, w["ffn2_norm"], w["ffn2_w_gate"], w["ffn2_w_up"], w["ffn2_w_down"],
                      w["final_norm"], layer=l, tm_p=tm_p, tm_s=tm_s, ff_split=ff_split, final_norm=(l == depth - 1))
    return (xp.reshape(bsz, seq, d), xs.reshape(dbsz, dseq, d),
            *_stream_outputs(p_acc, dims), *_stream_outputs(s_acc, dims))
```

```python
import functools

import jax
import jax.numpy as jnp
from jax import lax
from jax.experimental import pallas as pl
from jax.experimental.pallas import tpu as pltpu

EPS = 1e-6
LANES = 128
MXU_WIDTH = 256
BF16_ROWS = 16
MIN_SKEW_TILES = 8
WEIGHT_STAGE_BYTES = 1 << 20
VMEM_LIMIT_BYTES = 56 * 1024 * 1024

_BF16 = jnp.bfloat16
_F32 = jnp.float32


def _dot(a, b):
    return jnp.dot(a, b, preferred_element_type=_F32)


def _dot_nt(a, b):
    return lax.dot_general(a, b, (((1,), (1,)), ((), ())), preferred_element_type=_F32)


def _layer_spec(shape, layer):
    return pl.BlockSpec((None,) + shape, lambda *_: (layer, 0, 0), pipeline_mode=pl.Buffered(1))


def _ffn_kernel(xp_ref, xs_ref, g_ref, wg_hbm, wu_hbm, wd_hbm, fg_ref, op_ref, os_ref,
                wg_ref, wu_ref, wd_ref, stage_in, stage_out, sems,
                *, layer, n_p, final_norm, ff_chunks):
    i = pl.program_id(0)

    @pl.when(i == 0)
    def _fetch_weights():
        jobs = []
        for src, stage, dst in ((wg_hbm, stage_in, wg_ref), (wu_hbm, stage_in, wu_ref), (wd_hbm, stage_out, wd_ref)):
            rows = stage.shape[1]
            for r in range(0, dst.shape[0], rows):
                slot = len(jobs) % 2
                copy = pltpu.make_async_copy(src.at[layer, pl.ds(r, rows), :], stage.at[slot], sems.at[slot])
                jobs.append((copy, stage.at[slot], dst, r, rows))
        jobs[0][0].start()
        for k, (copy, slot_ref, dst, r, rows) in enumerate(jobs):
            if k + 1 < len(jobs):
                jobs[k + 1][0].start()
            copy.wait()
            dst[r:r + rows, :] = slot_ref[...].astype(_BF16)

    def run(x_ref, o_ref):
        x = x_ref[...]
        ms = jnp.mean(x * x, axis=-1, keepdims=True)
        hn = (x * lax.rsqrt(ms + EPS) * g_ref[...]).astype(_BF16)
        y = None
        for lo, hi in ff_chunks:
            gate = _dot(hn, wg_ref[:, lo:hi])
            up = _dot(hn, wu_ref[:, lo:hi])
            act = (gate * jax.nn.sigmoid(gate) * up).astype(_BF16)
            part = _dot(act, wd_ref[lo:hi, :])
            y = part if y is None else y + part
        y = x + 0.5 * y
        if final_norm:
            ms2 = jnp.mean(y * y, axis=-1, keepdims=True)
            y = y * lax.rsqrt(ms2 + EPS) * fg_ref[...]
        o_ref[...] = y

    @pl.when(i < n_p)
    def _prompt():
        run(xp_ref, op_ref)

    @pl.when(i >= n_p)
    def _sample():
        run(xs_ref, os_ref)


def _ff_chunks(dff, n_chunks):
    tiles = -(-dff // MXU_WIDTH)
    bounds = [min(dff, MXU_WIDTH * ((tiles * i) // n_chunks)) for i in range(n_chunks + 1)]
    return tuple((bounds[i], bounds[i + 1]) for i in range(n_chunks) if bounds[i + 1] > bounds[i])


def _staging_rows(n_rows, row_bytes):
    rows = BF16_ROWS
    while n_rows % (2 * rows) == 0 and 2 * rows * row_bytes <= WEIGHT_STAGE_BYTES:
        rows *= 2
    return rows


def _ffn(xp, xs, g, wg, wu, wd, fg, *, layer, tm_p, tm_s, ff_split, final_norm):
    d = xp.shape[1]
    dff = wg.shape[2]
    n_p, n_s = xp.shape[0] // tm_p, xs.shape[0] // tm_s
    prompt_tile = lambda i: (jnp.minimum(i, n_p - 1), 0)
    sample_tile = lambda i: (jnp.maximum(i - n_p, 0), 0)
    hbm = pl.BlockSpec(memory_space=pl.ANY)
    return pl.pallas_call(
        functools.partial(_ffn_kernel, layer=layer, n_p=n_p, final_norm=final_norm,
                          ff_chunks=_ff_chunks(dff, ff_split)),
        out_shape=(jax.ShapeDtypeStruct(xp.shape, _F32), jax.ShapeDtypeStruct(xs.shape, _F32)),
        grid=(n_p + n_s,),
        in_specs=[
            pl.BlockSpec((tm_p, d), prompt_tile),
            pl.BlockSpec((tm_s, d), sample_tile),
            _layer_spec((1, d), layer),
            hbm, hbm, hbm,
            pl.BlockSpec((1, d), lambda i: (0, 0)),
        ],
        out_specs=(pl.BlockSpec((tm_p, d), prompt_tile), pl.BlockSpec((tm_s, d), sample_tile)),
        scratch_shapes=[
            pltpu.VMEM((d, dff), _BF16),
            pltpu.VMEM((d, dff), _BF16),
            pltpu.VMEM((dff, d), _BF16),
            pltpu.VMEM((2, _staging_rows(d, 4 * dff), dff), _F32),
            pltpu.VMEM((2, _staging_rows(dff, 4 * d), d), _F32),
            pltpu.SemaphoreType.DMA((2,)),
        ],
        compiler_params=pltpu.CompilerParams(
            dimension_semantics=("arbitrary",), vmem_limit_bytes=VMEM_LIMIT_BYTES),
        name="ffn_final" if final_norm else "ffn",
    )(xp, xs, g, wg, wu, wd, fg)


def _log_sigmoid(x):
    return jnp.minimum(x, 0.0) - jnp.log1p(jnp.exp(-jnp.abs(x)))


def _pad_rows(a, rows):
    if a.shape[0] == rows:
        return a
    return jnp.concatenate([a, jnp.zeros((rows - a.shape[0], a.shape[1]), a.dtype)], axis=0)


def _project(xa_ref, nrm_ref, win_ref, wgate_ref, gbias_ref, p_ref, *, heads, dk, dv, conv_w):
    x = xa_ref[...]
    ms = jnp.mean(x * x, axis=-1, keepdims=True)
    hn = (x * lax.rsqrt(ms + EPS) * nrm_ref[...]).astype(_BF16)
    qk_w = heads * dk
    c0 = 3 * conv_w
    c1 = c0 + 2 * qk_w
    c2 = c1 + 2 * heads * dv
    p_ref[:, 0:c0] = _dot_nt(hn, win_ref[0:c0, :])
    p_ref[:, c0:c0 + qk_w] = _dot_nt(hn, win_ref[c0:c0 + qk_w, :]) * (dk ** -0.5)
    p_ref[:, c0 + qk_w:c1] = _dot_nt(hn, win_ref[c0 + qk_w:c1, :])
    p_ref[:, c1:c2] = _dot_nt(hn, win_ref[c1:c2, :])
    p_ref[:, c2:c2 + LANES] = _dot_nt(hn, wgate_ref[...]) + gbias_ref[...]


def _recurrent(p_ref, mix_ref, convw_ref, convb_ref, gconv_ref, gml_ref, ubuf, cn_scr, m_scr,
               *, tm, chunk, n_slots, heads, dk, dv, conv_w, group_dim):
    seg = tm // n_slots
    n_sub = seg // chunk
    lc = max(chunk, LANES)
    pairs = heads // 2
    qk_w = heads * dk
    c0 = 3 * conv_w
    ck = c0 + qk_w
    cv = c0 + 2 * qk_w
    co = cv + heads * dv
    cg = co + heads * dv

    lane = lax.broadcasted_iota(jnp.int32, (seg, LANES), 1)
    low_half = lane < group_dim
    for s in range(n_slots):
        r0 = s * seg
        u = p_ref[r0:r0 + seg, conv_w:2 * conv_w] * p_ref[r0:r0 + seg, 2 * conv_w:3 * conv_w]
        ubuf[s, 8:8 + seg, :] = u
        u1 = ubuf[s, 7:7 + seg, :]
        u2 = ubuf[s, 6:6 + seg, :]
        y = convb_ref[...] + convw_ref[0:1, :] * u2 + convw_ref[1:2, :] * u1 + convw_ref[2:3, :] * u
        ubuf[s, 6:8, :] = ubuf[s, seg + 6:seg + 8, :]
        a = p_ref[r0:r0 + seg, 0:conv_w] * y
        for jb in range(conv_w // LANES):
            blk = a[:, jb * LANES:(jb + 1) * LANES]
            sq = blk * blk
            lo = jnp.sum(jnp.where(low_half, sq, 0.0), axis=-1, keepdims=True)
            hi = jnp.sum(jnp.where(low_half, 0.0, sq), axis=-1, keepdims=True)
            msq = jnp.where(low_half, lo, hi) * (1.0 / group_dim)
            nb = blk * lax.rsqrt(msq + EPS) * gconv_ref[:, jb * LANES:(jb + 1) * LANES]
            mix_ref[r0:r0 + seg, jb * LANES:(jb + 1) * LANES] = nb.astype(_BF16)

    gates = p_ref[:, cg:cg + LANES]
    lf = _log_sigmoid(gates)
    rowc = lax.broadcasted_iota(jnp.int32, (tm, LANES), 0) % chunk
    bc = lf
    sh = 1
    while sh < chunk:
        bc = bc + jnp.where(rowc >= sh, pltpu.roll(bc, sh, axis=0), 0.0)
        sh *= 2
    c_all = gates - pltpu.roll(bc, LANES - heads, axis=1)

    row = lax.broadcasted_iota(jnp.int32, (chunk, lc), 0)
    col = lax.broadcasted_iota(jnp.int32, (chunk, lc), 1)
    causal_t = row <= col
    lane_c = lax.broadcasted_iota(jnp.int32, (lc, LANES), 1)
    head_lo = lane_c < dk
    lane_m = lax.broadcasted_iota(jnp.int32, (1, LANES), 1)
    real_t = lax.broadcasted_iota(jnp.int32, (1, lc), 1) < chunk
    ones_rows = jnp.where(lax.broadcasted_iota(jnp.int32, (BF16_ROWS, lc), 0) == 0, 1.0, 0.0)

    for s in range(n_slots):
        for j in range(n_sub):
            r0 = s * seg + j * chunk
            g_t = _pad_rows(gates[r0:r0 + chunk], lc).T
            b_t = _pad_rows(bc[r0:r0 + chunk], lc).T
            c_j = c_all[r0:r0 + chunk]
            m_all = m_scr[s]
            m_new_row = m_all
            for pr in range(pairs):
                q_pad = _pad_rows(p_ref[r0:r0 + chunk, c0 + pr * LANES:c0 + (pr + 1) * LANES], lc)
                k_pair = p_ref[r0:r0 + chunk, ck + pr * LANES:ck + (pr + 1) * LANES]
                k_bf = k_pair.astype(_BF16)
                k_pad = _pad_rows(k_pair, lc)
                cn_pair = cn_scr[s, pr]
                cn_bf = cn_pair.astype(_BF16)
                upd = None
                decays = []
                for hh in range(2):
                    h = 2 * pr + hh
                    hmask = head_lo if hh == 0 else jnp.logical_not(head_lo)
                    b_row = b_t[heads + h:heads + h + 1, :]
                    li_row = g_t[h:h + 1, :]
                    c_col = c_j[:, h:h + 1]
                    m_prev = m_all[:, h:h + 1]
                    d = jnp.where(causal_t, b_row + c_col, -jnp.inf)
                    inter = b_row + m_prev
                    m_t = jnp.maximum(inter, jnp.max(d, axis=0, keepdims=True))
                    w_intra = jnp.exp(d - m_t)
                    w_inter = jnp.exp(inter - m_t)
                    qm = jnp.where(hmask, q_pad, 0.0).astype(_BF16)
                    sc = _dot_nt(k_bf, qm) * w_intra
                    v_h = p_ref[r0:r0 + chunk, cv + h * dv:cv + (h + 1) * dv]
                    v_t = _pad_rows(v_h, lc).T
                    inter_cn = _dot_nt(cn_bf, qm)
                    num = _dot(v_t.astype(_BF16), _pad_rows(sc, lc).astype(_BF16)) + w_inter * inter_cn[0:dv]
                    den = jnp.sum(sc, axis=0, keepdims=True) + w_inter * inter_cn[dv:dv + 1]
                    h_t = num * (1.0 / jnp.maximum(jnp.abs(den), jnp.exp(-m_t)))
                    msq = jnp.mean(h_t * h_t, axis=0, keepdims=True)
                    h_n = (h_t * lax.rsqrt(msq + EPS)).T[0:chunk]
                    o_h = p_ref[r0:r0 + chunk, co + h * dv:co + (h + 1) * dv]
                    hm = h_n * gml_ref[:, h * dv:(h + 1) * dv] * jax.nn.sigmoid(o_h)
                    mix_ref[r0:r0 + chunk, conv_w + h * dv:conv_w + (h + 1) * dv] = hm.astype(_BF16)
                    b_last = b_row[:, chunk - 1:chunk]
                    m_new = m_t[:, chunk - 1:chunk]
                    wk = jnp.where(real_t, jnp.exp(b_last - b_row + li_row - m_new), 0.0)
                    decays.append(jnp.exp(b_last + m_prev - m_new))
                    vw = (jnp.concatenate([v_t, ones_rows], axis=0) * wk).astype(_BF16)
                    contrib = _dot(vw, jnp.where(hmask, k_pad, 0.0).astype(_BF16))
                    upd = contrib if upd is None else upd + contrib
                    m_new_row = jnp.where(lane_m == h, m_new, m_new_row)
                decay_lanes = jnp.where(lane_m < dk, decays[0], decays[1])
                cn_scr[s, pr] = decay_lanes * cn_pair + upd
            m_scr[s] = m_new_row


def _mixer_kernel(xa_ref, xc_ref, conv0_ref, c0_ref, n0_ref, m0_ref, nrm_ref, win_ref, wgate_ref, gbias_ref, convw_ref,
                  convb_ref, gconv_ref, gml_ref, wout_ref, conv_acc, c_acc, n_acc, m_acc,
                  xo_ref, convo_ref, co_ref, no_ref, mo_ref,
                  p0, p1, mix0, mix1, ubuf, cn_scr, m_scr,
                  *, skewed, n_tiles, n_t, tm, chunk, n_slots, heads, dk, dv, conv_w, group_dim):
    s = pl.program_id(0)
    lag = 1 if skewed else 0
    tile_b = jnp.clip(s - lag, 0, n_tiles - 1)
    t_b = tile_b % n_t
    valid_b = jnp.logical_and(s >= lag, s < n_tiles + lag)

    if skewed:
        @pl.when(s == 0)
        def _zero():
            p1[...] = jnp.zeros_like(p1)
            mix0[...] = jnp.zeros_like(mix0)
            ubuf[...] = jnp.zeros_like(ubuf)
            cn_scr[...] = jnp.zeros_like(cn_scr)
            m_scr[...] = jnp.zeros_like(m_scr)

    pairs = heads // 2

    @pl.when(jnp.logical_and(valid_b, t_b == 0))
    def _init():
        for i in range(n_slots):
            ubuf[i, 6:8, :] = conv0_ref[i]
            for pr in range(pairs):
                c_pair = c0_ref[i, 2 * pr:2 * pr + 2].reshape(2 * dk, dv)
                cn_scr[i, pr, 0:dv, :] = c_pair.T
                cn_scr[i, pr, dv:dv + BF16_ROWS, :] = _pad_rows(n0_ref[i, pr], BF16_ROWS)
        m_scr[...] = m0_ref[...]

    def step(p_w, p_r, mix_w, mix_r):
        _project(xa_ref, nrm_ref, win_ref, wgate_ref, gbias_ref, p_w, heads=heads, dk=dk, dv=dv, conv_w=conv_w)
        _recurrent(p_r, mix_w, convw_ref, convb_ref, gconv_ref, gml_ref, ubuf, cn_scr, m_scr,
                   tm=tm, chunk=chunk, n_slots=n_slots, heads=heads, dk=dk, dv=dv, conv_w=conv_w,
                   group_dim=group_dim)
        xo_ref[...] = xc_ref[...] + _dot(mix_r[...], wout_ref[...])

    if skewed:
        @pl.when(s % 2 == 0)
        def _even():
            step(p0, p1, mix1, mix0)

        @pl.when(s % 2 == 1)
        def _odd():
            step(p1, p0, mix0, mix1)
    else:
        step(p0, p0, mix0, mix0)

    @pl.when(jnp.logical_and(valid_b, t_b == n_t - 1))
    def _fin():
        for i in range(n_slots):
            convo_ref[i] = ubuf[i, 6:8, :]
            for pr in range(pairs):
                co_ref[i, 2 * pr:2 * pr + 2] = cn_scr[i, pr, 0:dv, :].T.reshape(2, dk, dv)
                no_ref[i, pr] = cn_scr[i, pr, dv:dv + 1, :]
        mo_ref[...] = m_scr[...]


def _mixer(x, conv0, c0, n0, m0, nrm, win, wgate, gbias, convw, convb, gconv, gml, wout, acc,
           *, layer, n_seq, seq_len, tm, chunk, n_slots, heads, dk, dv, conv_w, group_dim):
    n, d = x.shape
    seg = tm // n_slots
    n_t = seq_len // seg
    n_tiles = n // tm
    pairs = heads // 2
    n_main = win.shape[1] // LANES * LANES
    nin = n_main + LANES
    skewed = n_tiles >= MIN_SKEW_TILES
    lag_b, lag_c = (1, 2) if skewed else (0, 0)
    kern = functools.partial(_mixer_kernel, skewed=skewed, n_tiles=n_tiles, n_t=n_t, tm=tm, chunk=chunk,
                             n_slots=n_slots, heads=heads, dk=dk, dv=dv, conv_w=conv_w,
                             group_dim=group_dim)
    tile_a = lambda s: (jnp.minimum(s, n_tiles - 1), 0)
    tile_c = lambda s: (jnp.maximum(s - lag_c, 0), 0)
    seq_b = lambda s: jnp.clip(s - lag_b, 0, n_tiles - 1) // n_t
    state_specs = [
        pl.BlockSpec((None, n_slots, 2, conv_w), lambda s: (layer, seq_b(s), 0, 0)),
        pl.BlockSpec((None, n_slots, heads, dk, dv), lambda s: (layer, seq_b(s), 0, 0, 0)),
        pl.BlockSpec((None, n_slots, pairs, 1, 2 * dk), lambda s: (layer, seq_b(s), 0, 0, 0)),
        pl.BlockSpec((None, n_slots, 1, LANES), lambda s: (layer, seq_b(s), 0, 0)),
    ]
    operands = (x, x, conv0, c0, n0, m0, nrm, win, wgate, gbias, convw, convb, gconv, gml, wout)
    return pl.pallas_call(
        kern,
        out_shape=(jax.ShapeDtypeStruct((n, d), _F32),) + tuple(
            jax.ShapeDtypeStruct(a.shape, a.dtype) for a in acc),
        grid=(n_tiles + lag_c,),
        in_specs=[
            pl.BlockSpec((tm, d), tile_a),
            pl.BlockSpec((tm, d), tile_c),
            *state_specs,
            _layer_spec((1, d), layer),
            _layer_spec((n_main, d), layer),
            _layer_spec((LANES, d), layer),
            _layer_spec((1, LANES), layer),
            _layer_spec((3, conv_w), layer),
            _layer_spec((1, conv_w), layer),
            _layer_spec((1, conv_w), layer),
            _layer_spec((1, heads * dv), layer),
            _layer_spec((conv_w + heads * dv, d), layer),
            *[pl.BlockSpec(memory_space=pl.ANY)] * len(acc),
        ],
        out_specs=(pl.BlockSpec((tm, d), tile_c), *state_specs),
        input_output_aliases={len(operands) + i: 1 + i for i in range(len(acc))},
        scratch_shapes=[
            pltpu.VMEM((tm, nin), _F32),
            pltpu.VMEM((tm, nin), _F32),
            pltpu.VMEM((tm, conv_w + heads * dv), _BF16),
            pltpu.VMEM((tm, conv_w + heads * dv), _BF16),
            pltpu.VMEM((n_slots, seg + 8, conv_w), _F32),
            pltpu.VMEM((n_slots, pairs, dv + BF16_ROWS, 2 * dk), _F32),
            pltpu.VMEM((n_slots, 1, LANES), _F32),
        ],
        compiler_params=pltpu.CompilerParams(
            dimension_semantics=("arbitrary",), vmem_limit_bytes=VMEM_LIMIT_BYTES),
        name="mixer",
    )(*operands, *acc)


def _prep_weights(ffn1_norm, ffn1_w_gate, ffn1_w_up, ffn1_w_down, mix_norm, w_in, b_igate, b_fgate,
                  conv_w, conv_b, g_conv, g_mlstm, w_out, ffn2_norm, ffn2_w_gate, ffn2_w_up,
                  ffn2_w_down, final_norm):
    depth, d, _ = w_in.shape
    heads = b_igate.shape[1]
    n_main = w_in.shape[2] - 2 * heads
    w_in_t = jnp.swapaxes(w_in, 1, 2)
    w_main = w_in_t.astype(_BF16)
    w_gate = jnp.concatenate(
        [w_in_t[:, n_main:, :].astype(_BF16), jnp.zeros((depth, LANES - 2 * heads, d), _BF16)], axis=1)
    gbias = jnp.concatenate(
        [b_igate, b_fgate, jnp.zeros((depth, LANES - 2 * heads), _F32)], axis=-1)[:, None, :]
    return dict(
        ffn1_norm=ffn1_norm[:, None, :], ffn1_w_gate=ffn1_w_gate,
        ffn1_w_up=ffn1_w_up, ffn1_w_down=ffn1_w_down,
        mix_norm=mix_norm[:, None, :], w_in=w_main, w_gate=w_gate, gbias=gbias, conv_w=conv_w,
        conv_b=conv_b[:, None, :], g_conv=g_conv.reshape(depth, 1, -1),
        g_mlstm=g_mlstm.reshape(depth, 1, -1), w_out=w_out.astype(_BF16),
        ffn2_norm=ffn2_norm[:, None, :], ffn2_w_gate=ffn2_w_gate,
        ffn2_w_up=ffn2_w_up, ffn2_w_down=ffn2_w_down,
        final_norm=final_norm[None, :],
    )


def _tile_plan(n_seq, seq_len):
    if seq_len >= 256:
        chunk = 256
        tm = 512 if seq_len % 512 == 0 else 256
        return tm, chunk, 1
    chunk = seq_len
    n_slots = max(1, min(n_seq, 256 // seq_len))
    while n_seq % n_slots:
        n_slots -= 1
    return n_slots * seq_len, chunk, n_slots


def _stream_state(conv0, c0, n0, m0, dims):
    heads, dk = dims[0], dims[1]
    depth, n_seq = m0.shape[0], m0.shape[1]
    n0r = n0.reshape(depth, n_seq, heads // 2, 1, 2 * dk)
    m0p = jnp.concatenate([m0, jnp.zeros((depth, n_seq, LANES - heads), _F32)], axis=-1)[:, :, None, :]
    state = (conv0, c0, n0r, m0p)
    return state, tuple(jnp.zeros(a.shape, _F32) for a in state)


def _stream_outputs(acc, dims):
    heads, dk = dims[0], dims[1]
    conv_o, c_o, n_o, m_o = acc
    depth, n_seq = m_o.shape[0], m_o.shape[1]
    return conv_o, c_o, n_o.reshape(depth, n_seq, heads, dk), m_o[:, :, 0, :heads]


def kernel(x_prompt, x_sample, cache_conv, state_C, state_n, state_m, ffn1_norm, ffn1_w_gate, ffn1_w_up, ffn1_w_down, mix_norm, w_in, b_igate, b_fgate, conv_w, conv_b, g_conv, g_mlstm, w_out, ffn2_norm, ffn2_w_gate, ffn2_w_up, ffn2_w_down, final_norm):
    depth, _, heads, dk, dv = state_C.shape
    cw = conv_w.shape[2]
    group_dim = g_conv.shape[2]
    assert heads % 2 == 0 and 2 * dk == LANES and dv == LANES and 2 * group_dim == LANES
    dims = (heads, dk, dv, cw, group_dim)
    w = _prep_weights(ffn1_norm, ffn1_w_gate, ffn1_w_up, ffn1_w_down, mix_norm, w_in, b_igate, b_fgate,
                      conv_w, conv_b, g_conv, g_mlstm, w_out, ffn2_norm, ffn2_w_gate, ffn2_w_up,
                      ffn2_w_down, final_norm)
    bsz, seq, d = x_prompt.shape
    dbsz, dseq, _ = x_sample.shape
    p_state, p_acc = _stream_state(
        jnp.zeros((depth, bsz, 2, cw), _F32), jnp.zeros((depth, bsz, heads, dk, dv), _F32),
        jnp.zeros((depth, bsz, heads, dk), _F32), jnp.zeros((depth, bsz, heads), _F32), dims)
    s_state, s_acc = _stream_state(cache_conv, state_C, state_n, state_m, dims)
    p_plan = _tile_plan(bsz, seq)
    s_plan = _tile_plan(dbsz, dseq)
    xp = x_prompt.reshape(bsz * seq, d)
    xs = x_sample.reshape(dbsz * dseq, d)
    tm_p = 1024 if xp.shape[0] % 1024 == 0 else p_plan[0]
    tm_s = 256 if xs.shape[0] % 256 == 0 else s_plan[0]
    ff_split = 3 if tm_p >= 1024 else 1

    def mixer(x, state, acc, plan, n_seq, seq_len, l):
        tm, chunk, n_slots = plan
        x, *acc = _mixer(
            x, *state, w["mix_norm"], w["w_in"], w["w_gate"], w["gbias"], w["conv_w"], w["conv_b"],
            w["g_conv"], w["g_mlstm"], w["w_out"], acc,
            layer=l, n_seq=n_seq, seq_len=seq_len, tm=tm, chunk=chunk, n_slots=n_slots, heads=heads,
            dk=dk, dv=dv, conv_w=cw, group_dim=group_dim)
        return x, tuple(acc)

    for l in range(depth):
        xp, xs = _ffn(xp, xs, w["ffn1_norm"], w["ffn1_w_gate"], w["ffn1_w_up"], w["ffn1_w_down"],
                      w["final_norm"], layer=l, tm_p=tm_p, tm_s=tm_s, ff_split=ff_split, final_norm=False)
        xp, p_acc = mixer(xp, p_state, p_acc, p_plan, bsz, seq, l)
        xs, s_acc = mixer(xs, s_state, s_acc, s_plan, dbsz, dseq, l)
        xp, xs = _ffn(xp, xs, w["ffn2_norm"], w["ffn2_w_gate"], w["ffn2_w_up"], w["ffn2_w_down"],
                      w["final_norm"], layer=l, tm_p=tm_p, tm_s=tm_s, ff_split=ff_split, final_norm=(l == depth - 1))
    return (xp.reshape(bsz, seq, d), xs.reshape(dbsz, dseq, d),
            *_stream_outputs(p_acc, dims), *_stream_outputs(s_acc, dims))
```
